```python
import math
import jax, jax.numpy as jnp
from jax import lax
import numpy as np

D_MODEL = 1024
BATCH = 32
SEQ = 256
DEPTH = 2
DEC_BATCH = 4
DEC_SEQ = 4096
PAST_LEN = 256

GRID_W = 64
HEAD_DIM = 64
NA_HEADS = (3 * D_MODEL // 8) // HEAD_DIM
NA_WIN_H = 8
NA_WIN_W = 16
LRU_WIDTH = D_MODEL // 4
LRU_BLOCKS = 4
LRU_BW = LRU_WIDTH // LRU_BLOCKS
LRU_C = 8.0
CONV_W = 4
GQA_HEADS = (D_MODEL - NA_HEADS * HEAD_DIM - LRU_WIDTH) // HEAD_DIM
GQA_KV_HEADS = 2
N_EXPERTS = 16
EXPERT_FF = 1024
EC_FACTOR = 2
ROPE_THETA = 10000.0
Q_BLOCK = 128
EPS = 1e-6

NA_DIM = NA_HEADS * HEAD_DIM
GQA_Q_DIM = GQA_HEADS * HEAD_DIM
GQA_KV_DIM = GQA_KV_HEADS * HEAD_DIM
MIX_DIM = NA_DIM + LRU_WIDTH + GQA_Q_DIM
IN_SPLITS = [NA_DIM, NA_DIM, NA_DIM, LRU_WIDTH, LRU_WIDTH, GQA_Q_DIM, GQA_KV_DIM, GQA_KV_DIM]
IN_DIM = sum(IN_SPLITS)
IN_OFFSETS = [int(o) for o in np.cumsum(IN_SPLITS)[:-1]]

kernel_name = "hybrid_diffusion_na_rglru_gqa_ec"

F32 = jnp.float32


def rms_norm(x, g):
    x32 = x.astype(F32)
    y = x32 * lax.rsqrt(jnp.mean(x32 * x32, axis=-1, keepdims=True) + EPS)
    return (y * g.astype(F32)).astype(x.dtype)


def split_heads(t, n_heads):
    b, s, _ = t.shape
    return t.reshape(b, s, n_heads, HEAD_DIM).transpose(0, 2, 1, 3)


def merge_heads(t):
    b, h, s, d = t.shape
    return t.transpose(0, 2, 1, 3).reshape(b, s, h * d)


def rotate_pairs(x, ang):
    x1, x2 = jnp.split(x, 2, axis=-1)
    cs = jnp.cos(ang).astype(x.dtype)
    sn = jnp.sin(ang).astype(x.dtype)
    return jnp.concatenate([x1 * cs - x2 * sn, x2 * cs + x1 * sn], axis=-1)


def axial_rope(x):
    t = x.shape[2]
    pos = jnp.arange(t)
    n = HEAD_DIM // 4
    inv = ROPE_THETA ** (-jnp.arange(n, dtype=F32) / n)
    ang_r = (pos // GRID_W).astype(F32)[:, None] * inv[None, :]
    ang_c = (pos % GRID_W).astype(F32)[:, None] * inv[None, :]
    xr, xc = jnp.split(x, 2, axis=-1)
    return jnp.concatenate([rotate_pairs(xr, ang_r), rotate_pairs(xc, ang_c)], axis=-1)


def block_attention(q, k, v):
    b, hq, tq, dh = q.shape
    hk = k.shape[1]
    g = hq // hk
    nb = tq // Q_BLOCK
    scale = dh ** -0.5
    qb = q.reshape(b, hk, g, nb, Q_BLOCK, dh).transpose(3, 0, 1, 2, 4, 5)

    def one(qblk):
        s = jnp.einsum('bkgqd,bktd->bkgqt', qblk, k).astype(F32) * scale
        p = jax.nn.softmax(s, axis=-1).astype(v.dtype)
        return jnp.einsum('bkgqt,bktd->bkgqd', p, v)

    o = lax.map(one, qb)
    return o.transpose(1, 2, 3, 0, 4, 5).reshape(b, hq, tq, dh)


def neighbourhood_attention(q, k, v, kc, vc, rpb):
    b, h, t, dh = q.shape
    rows = t // GRID_W
    kh = min(NA_WIN_H, rows)
    kw = NA_WIN_W
    nk = kh * kw
    scale = dh ** -0.5
    cols = jnp.arange(GRID_W)
    col_start = jnp.clip(cols - kw // 2, 0, GRID_W - kw)
    key_cols = col_start[:, None] + jnp.arange(kw)[None, :]
    dc = key_cols - cols[:, None] + (NA_WIN_W - 1)

    def one(r):
        row_start = jnp.clip(r - kh // 2, 0, rows - kh)
        key_rows = row_start + jnp.arange(kh)
        idx = (key_rows[None, :, None] * GRID_W + key_cols[:, None, :]).reshape(GRID_W, nk)
        dr = key_rows - r + (NA_WIN_H - 1)
        bias = rpb[:, dr[None, :, None], dc[:, None, :]].reshape(h, GRID_W, nk)
        qr = lax.dynamic_slice_in_dim(q, r * GRID_W, GRID_W, axis=2)
        kg = jnp.take(k, idx, axis=2)
        vg = jnp.take(v, idx, axis=2)
        s_win = jnp.einsum('bhqd,bhqnd->bhqn', qr, kg).astype(F32) * scale + bias.astype(F32)[None]
        s_ctx = jnp.einsum('bhqd,bhld->bhql', qr, kc).astype(F32) * scale
        p = jax.nn.softmax(jnp.concatenate([s_win, s_ctx], axis=-1), axis=-1).astype(v.dtype)
        return (jnp.einsum('bhqn,bhqnd->bhqd', p[..., :nk], vg)
                + jnp.einsum('bhql,bhld->bhqd', p[..., nk:], vc))

    o = lax.map(one, jnp.arange(rows))
    return o.transpose(1, 2, 0, 3, 4).reshape(b, h, t, dh)


def centred_dwconv(x, w, bias):
    t = x.shape[1]
    left = CONV_W // 2
    right = CONV_W - 1 - left
    xp = jnp.pad(x, ((0, 0), (left, right), (0, 0)))
    y = bias
    for j in range(CONV_W):
        y = y + xp[:, j:j + t] * w[j]
    return y


def _lin_combine(e1, e2):
    a1, b1 = e1
    a2, b2 = e2
    return a1 * a2, a2 * b1 + b2


def rglru(x, h0, wa, ba, wi, bi, lam, reverse):
    b, t, w = x.shape
    xb = x.reshape(b, t, LRU_BLOCKS, LRU_BW)
    r = jax.nn.sigmoid(jnp.einsum('btnd,nde->btne', xb, wa).reshape(b, t, w) + ba)
    i = jax.nn.sigmoid(jnp.einsum('btnd,nde->btne', xb, wi).reshape(b, t, w) + bi)
    log_a = (-LRU_C * jax.nn.softplus(-lam.astype(F32))) * r.astype(F32)
    a = jnp.exp(log_a)
    u = jnp.sqrt(-jnp.expm1(2.0 * log_a)) * (i * x)
    start = t - 1 if reverse else 0
    end = 0 if reverse else t - 1
    u = u.at[:, start].add(a[:, start] * h0.astype(F32))
    _, hs = lax.associative_scan(_lin_combine, (a, u), reverse=reverse, axis=1)
    return hs, hs[:, end]


def recurrent_branch(xr, gate, h0_f, h0_b, P):
    xc = centred_dwconv(xr, P['conv_w'], P['conv_b']).astype(F32)
    hf, sf = rglru(xc, h0_f, P['lru_w_a'][0], P['lru_b_a'][0], P['lru_w_i'][0], P['lru_b_i'][0],
                   P['lru_lambda'][0], False)
    hb, sb = rglru(xc, h0_b, P['lru_w_a'][1], P['lru_b_a'][1], P['lru_w_i'][1], P['lru_b_i'][1],
                   P['lru_lambda'][1], True)
    y = jax.nn.gelu(gate.astype(F32)) * (hf + hb)
    return y.astype(xr.dtype), jnp.stack([sf, sb], axis=1)


def modulation(cvec, P):
    m = jax.nn.silu(cvec) @ P['mod_w'] + P['mod_b']
    return jnp.split(m[:, None, :], 6, axis=-1)


def expert_choice_ffn(u, P):
    b, t, d = u.shape
    cap = EC_FACTOR * t // N_EXPERTS
    aff = jax.nn.softmax(jnp.einsum('btd,de->bte', u, P['router_w']).astype(F32), axis=-1)
    g, idx = lax.top_k(jnp.swapaxes(aff, 1, 2), cap)
    xe = jax.vmap(lambda ub, ib: ub[ib])(u, idx)
    hmid = (jax.nn.silu(jnp.einsum('becd,edf->becf', xe, P['expert_w_gate']))
            * jnp.einsum('becd,edf->becf', xe, P['expert_w_up']))
    ye = jnp.einsum('becf,efd->becd', hmid, P['expert_w_down']) * g[..., None].astype(u.dtype)
    out = jax.vmap(lambda ib, yb: jnp.zeros((t, d), yb.dtype).at[ib.reshape(-1)].add(yb.reshape(-1, d)))(idx, ye)
    return out.astype(u.dtype)


def project_heads(u, P):
    qa, ka, va, xr, gr, qc, kc, vc = jnp.split(u @ P['w_in'], IN_OFFSETS, axis=-1)
    qa = rms_norm(split_heads(qa, NA_HEADS), P['na_q_norm_g'])
    ka = rms_norm(split_heads(ka, NA_HEADS), P['na_k_norm_g'])
    va = split_heads(va, NA_HEADS)
    qc = rms_norm(split_heads(qc, GQA_HEADS), P['gqa_q_norm_g'])
    kc = rms_norm(split_heads(kc, GQA_KV_HEADS), P['gqa_k_norm_g'])
    vc = split_heads(vc, GQA_KV_HEADS)
    return qa, ka, va, xr, gr, qc, kc, vc


def channel_sublayer(x, sh2, sc2, g2, P):
    u = rms_norm(x, P['norm_ffn_g']) * (1 + sc2) + sh2
    return x + g2 * expert_choice_ffn(u, P)


def context_layer(x, c_ctx, P):
    sh1, sc1, g1, sh2, sc2, g2 = modulation(c_ctx[None, :], P)
    u = rms_norm(x, P['norm_mix_g']) * (1 + sc1) + sh1
    qa, ka, va, xr, gr, qc, kc, vc = project_heads(u, P)
    oa = block_attention(qa, ka, va)
    zeros = jnp.zeros((x.shape[0], LRU_WIDTH), F32)
    ob, lru_state = recurrent_branch(xr, gr, zeros, zeros, P)
    oc = block_attention(qc, kc, vc)
    mix = jnp.concatenate([merge_heads(oa), ob, merge_heads(oc)], axis=-1) @ P['w_out']
    x = x + g1 * mix
    x = channel_sublayer(x, sh2, sc2, g2, P)
    return x, (ka, va, kc, vc, lru_state)


def latent_layer(x, c, P, ctx_na_k, ctx_na_v, ctx_gqa_k, ctx_gqa_v, ctx_lru):
    sh1, sc1, g1, sh2, sc2, g2 = modulation(c, P)
    u = rms_norm(x, P['norm_mix_g']) * (1 + sc1) + sh1
    qa, ka, va, xr, gr, qc, kc, vc = project_heads(u, P)
    oa = neighbourhood_attention(qa, ka, va, ctx_na_k.astype(ka.dtype), ctx_na_v.astype(va.dtype), P['na_rpb'])
    ob, _ = recurrent_branch(xr, gr, ctx_lru[:, 0], ctx_lru[:, 1], P)
    k_all = jnp.concatenate([axial_rope(kc), ctx_gqa_k.astype(kc.dtype)], axis=2)
    v_all = jnp.concatenate([vc, ctx_gqa_v.astype(vc.dtype)], axis=2)
    oc = block_attention(axial_rope(qc), k_all, v_all)
    mix = jnp.concatenate([merge_heads(oa), ob, merge_heads(oc)], axis=-1) @ P['w_out']
    x = x + g1 * mix
    return channel_sublayer(x, sh2, sc2, g2, P)


def setup_inputs(seed: int = 0) -> dict:
    key = jax.random.key(seed)
    ks = iter(jax.random.split(key, 40))

    def nrm(shape, scale):
        return jax.random.normal(next(ks), shape, F32) * scale

    L = DEPTH
    D = D_MODEL
    u = jax.random.uniform(next(ks), (L, 2, LRU_WIDTH), F32, 0.9, 0.999)
    a0 = u ** (1.0 / LRU_C)
    lam = jnp.log(a0) - jnp.log1p(-a0)
    return {
        'x_prompt': nrm((BATCH, SEQ, D), 1.0),
        'x_sample': nrm((DEC_BATCH, DEC_SEQ, D), 1.0),
        'c': nrm((DEC_BATCH, D), 1.0),
        'cache_na_k': nrm((DEC_BATCH, L, NA_HEADS, PAST_LEN, HEAD_DIM), 1.0),
        'cache_na_v': nrm((DEC_BATCH, L, NA_HEADS, PAST_LEN, HEAD_DIM), 1.0),
        'cache_gqa_k': nrm((DEC_BATCH, L, GQA_KV_HEADS, PAST_LEN, HEAD_DIM), 1.0),
        'cache_gqa_v': nrm((DEC_BATCH, L, GQA_KV_HEADS, PAST_LEN, HEAD_DIM), 1.0),
        'state_lru': nrm((DEC_BATCH, L, 2, LRU_WIDTH), 0.5),
        'c_ctx': nrm((D,), 1.0),
        'mod_w': nrm((L, D, 6 * D), 0.5 * D ** -0.5),
        'mod_b': nrm((L, 6 * D), 0.1),
        'norm_mix_g': 1.0 + nrm((L, D), 0.1),
        'norm_ffn_g': 1.0 + nrm((L, D), 0.1),
        'w_in': nrm((L, D, IN_DIM), D ** -0.5),
        'na_q_norm_g': 1.0 + nrm((L, HEAD_DIM), 0.1),
        'na_k_norm_g': 1.0 + nrm((L, HEAD_DIM), 0.1),
        'na_rpb': nrm((L, NA_HEADS, 2 * NA_WIN_H - 1, 2 * NA_WIN_W - 1), 0.1),
        'conv_w': nrm((L, CONV_W, LRU_WIDTH), CONV_W ** -0.5),
        'conv_b': nrm((L, LRU_WIDTH), 0.05),
        'lru_w_a': nrm((L, 2, LRU_BLOCKS, LRU_BW, LRU_BW), LRU_BW ** -0.5),
        'lru_b_a': nrm((L, 2, LRU_WIDTH), 0.1),
        'lru_w_i': nrm((L, 2, LRU_BLOCKS, LRU_BW, LRU_BW), LRU_BW ** -0.5),
        'lru_b_i': nrm((L, 2, LRU_WIDTH), 0.1),
        'lru_lambda': lam,
        'gqa_q_norm_g': 1.0 + nrm((L, HEAD_DIM), 0.1),
        'gqa_k_norm_g': 1.0 + nrm((L, HEAD_DIM), 0.1),
        'w_out': nrm((L, MIX_DIM, D), MIX_DIM ** -0.5),
        'router_w': nrm((L, D, N_EXPERTS), D ** -0.5),
        'expert_w_gate': nrm((L, N_EXPERTS, D, EXPERT_FF), D ** -0.5),
        'expert_w_up': nrm((L, N_EXPERTS, D, EXPERT_FF), D ** -0.5),
        'expert_w_down': nrm((L, N_EXPERTS, EXPERT_FF, D), EXPERT_FF ** -0.5),
    }


def reference(x_prompt, x_sample, c, cache_na_k, cache_na_v, cache_gqa_k, cache_gqa_v, state_lru, c_ctx,
              mod_w, mod_b, norm_mix_g, norm_ffn_g, w_in, na_q_norm_g, na_k_norm_g, na_rpb,
              conv_w, conv_b, lru_w_a, lru_b_a, lru_w_i, lru_b_i, lru_lambda,
              gqa_q_norm_g, gqa_k_norm_g, w_out, router_w, expert_w_gate, expert_w_up, expert_w_down):
    y_p = x_prompt
    y_s = x_sample
    na_k_l, na_v_l, gqa_k_l, gqa_v_l, lru_l = [], [], [], [], []
    for l in range(DEPTH):
        P = {
            'mod_w': mod_w[l], 'mod_b': mod_b[l], 'norm_mix_g': norm_mix_g[l], 'norm_ffn_g': norm_ffn_g[l],
            'w_in': w_in[l], 'na_q_norm_g': na_q_norm_g[l], 'na_k_norm_g': na_k_norm_g[l], 'na_rpb': na_rpb[l],
            'conv_w': conv_w[l], 'conv_b': conv_b[l], 'lru_w_a': lru_w_a[l], 'lru_b_a': lru_b_a[l],
            'lru_w_i': lru_w_i[l], 'lru_b_i': lru_b_i[l], 'lru_lambda': lru_lambda[l],
            'gqa_q_norm_g': gqa_q_norm_g[l], 'gqa_k_norm_g': gqa_k_norm_g[l], 'w_out': w_out[l],
            'router_w': router_w[l], 'expert_w_gate': expert_w_gate[l], 'expert_w_up': expert_w_up[l],
            'expert_w_down': expert_w_down[l],
        }
        y_p, (nk, nv, gk, gv, st) = context_layer(y_p, c_ctx, P)
        na_k_l.append(nk)
        na_v_l.append(nv)
        gqa_k_l.append(gk)
        gqa_v_l.append(gv)
        lru_l.append(st)
        y_s = latent_layer(y_s, c, P, cache_na_k[:, l], cache_na_v[:, l], cache_gqa_k[:, l],
                           cache_gqa_v[:, l], state_lru[:, l])
    new_na_k = jnp.stack(na_k_l, axis=1)
    new_na_v = jnp.stack(na_v_l, axis=1)
    new_gqa_k = jnp.stack(gqa_k_l, axis=1)
    new_gqa_v = jnp.stack(gqa_v_l, axis=1)
    new_lru = jnp.stack(lru_l, axis=1)
    return (y_p, y_s, new_na_k, new_na_v, new_gqa_k, new_gqa_v, new_lru)
```

```python
import functools

import jax
import jax.numpy as jnp
from jax import lax
from jax.experimental import pallas as pl
from jax.experimental.pallas import tpu as pltpu
from jax.experimental.pallas import tpu_sc as plsc

F32 = jnp.float32
BF16 = jnp.bfloat16
I32 = jnp.int32

HEAD_DIM = 64
NA_HEADS = 6
GQA_HEADS = 6
GQA_KV_HEADS = 2
GQA_GROUP = GQA_HEADS // GQA_KV_HEADS
LRU_WIDTH = 256
LRU_C = 8.0
CONV_W = 4
GRID_W = 64
NA_WIN_H = 8
NA_WIN_W = 16
N_EXPERTS = 16
EC_FACTOR = 2
ROPE_THETA = 10000.0
EPS = 1e-6
NA_DIM = NA_HEADS * HEAD_DIM
GQA_Q_DIM = GQA_HEADS * HEAD_DIM
GQA_KV_DIM = GQA_KV_HEADS * HEAD_DIM
OFF_QA = 0
OFF_KA = OFF_QA + NA_DIM
OFF_VA = OFF_KA + NA_DIM
OFF_XR = OFF_VA + NA_DIM
OFF_GR = OFF_XR + LRU_WIDTH
OFF_QC = OFF_GR + LRU_WIDTH
OFF_KC = OFF_QC + GQA_Q_DIM
OFF_VC = OFF_KC + GQA_KV_DIM

LANES = 128
SUBLANES = 8
VMEM_LIMIT_BYTES = 56 * 1024 * 1024

TOKEN_TILE = 512
TOKEN_SUBTILE = 256
INPROJ_TILE = 256
NA_ROW_BLOCK = 4
NA_KEY_ROWS = NA_ROW_BLOCK + NA_WIN_H
NA_BLOCKS_PER_STEP = 4
NA_Q_SPLIT = 1
GQA_Q_TILE = 256
GQA_KEY_CHUNK = 1024
FFN_ROW_TILE = 256
MASK_VALUE = -1e30


def _cparams(*sem, flags=None):
    return pltpu.CompilerParams(dimension_semantics=sem, vmem_limit_bytes=VMEM_LIMIT_BYTES, flags=flags)


def _dot(a, b):
    return jnp.dot(a, b, preferred_element_type=F32)


def _dot_nt(a, b):
    return lax.dot_general(a, b, (((1,), (1,)), ((), ())), preferred_element_type=F32)


def _rms(x, g):
    ms = jnp.mean(x * x, axis=-1, keepdims=True)
    return x * lax.rsqrt(ms + EPS) * g


def _full(a):
    nd = a.ndim
    return pl.BlockSpec(a.shape, lambda *_: (0,) * nd)


def _mod_spec(mod, tiles_per_request):
    blk = (None, 1, mod.shape[-1])
    if mod.shape[0] == 1:
        return pl.BlockSpec(blk, lambda i: (0, 0, 0))
    return pl.BlockSpec(blk, lambda i: (i // tiles_per_request, 0, 0))


def _mod_kernel(c_ref, w_ref, b_ref, o_ref):
    s = jax.nn.silu(c_ref[...])
    o_ref[...] = jnp.dot(s, w_ref[...], precision=lax.Precision.HIGHEST,
                         preferred_element_type=F32) + b_ref[...]


def _modulation(cvec, mod_w, mod_b):
    depth, d, n = mod_w.shape
    rows = cvec.shape[0]
    tn = 1536
    return pl.pallas_call(
        _mod_kernel,
        grid=(depth, n // tn),
        in_specs=[
            pl.BlockSpec((rows, d), lambda l, j: (0, 0)),
            pl.BlockSpec((None, d, tn), lambda l, j: (l, 0, j)),
            pl.BlockSpec((None, 1, tn), lambda l, j: (l, 0, j)),
        ],
        out_specs=pl.BlockSpec((None, rows, tn), lambda l, j: (l, 0, j)),
        out_shape=jax.ShapeDtypeStruct((depth, rows, n), F32),
        compiler_params=_cparams("arbitrary", "arbitrary"),
        name="modulation",
    )(cvec, mod_w, mod_b.reshape(depth, 1, n))


def _pair_rms(z, gain, lo):
    z2 = z * z
    s_lo = jnp.sum(jnp.where(lo, z2, 0.0), axis=-1, keepdims=True)
    s_hi = jnp.sum(jnp.where(lo, 0.0, z2), axis=-1, keepdims=True)
    r = jnp.where(lo, lax.rsqrt(s_lo * (1.0 / HEAD_DIM) + EPS), lax.rsqrt(s_hi * (1.0 / HEAD_DIM) + EPS))
    return z * r * gain


def _pair_rope(z, cos, sin_signed, even_blk):
    nxt = pltpu.roll(z, LANES - 16, 1)
    prv = pltpu.roll(z, 16, 1)
    return z * cos + jnp.where(even_blk, nxt, prv) * sin_signed


def _inproj_kernel(has_prev, rope, kv_per_request, *refs):
    if has_prev:
        f_ref, mprev_ref, refs = refs[0], refs[1], refs[2:]
    x_ref, mod_ref, g_ref, w_ref, gains_ref = refs[:5]
    refs = refs[5:]
    if rope:
        cos_ref, sin_ref, refs = refs[0], refs[1], refs[2:]
    if has_prev:
        x_out, refs = refs[0], refs[1:]
    qa_o, ka_o, va_o, xr_o, gr_o, qc_o, kc_o, vc_o = refs
    d = x_ref.shape[-1]
    lane = lax.broadcasted_iota(I32, (1, LANES), 1)
    lo = lane < HEAD_DIM
    even_blk = (lane // 16) % 2 == 0
    scale = HEAD_DIM ** -0.5
    sh1 = mod_ref[:, 0:d]
    sc1 = mod_ref[:, d:2 * d]

    for sub in range(x_ref.shape[0] // TOKEN_SUBTILE):
        rows = pl.ds(sub * TOKEN_SUBTILE, TOKEN_SUBTILE)
        x = x_ref[rows, :]
        if has_prev:
            x = x + mprev_ref[:, 5 * d:6 * d] * f_ref[rows, :]
            x_out[rows, :] = x
        u = _rms(x, g_ref[...]) * (1.0 + sc1) + sh1
        y = _dot(u.astype(BF16), w_ref[...])

        def heads_out(o_ref, off, n_pairs, gain_row, use_rope, mul, dtype, per_request):
            for p in range(n_pairs):
                z = y[:, off + p * LANES: off + (p + 1) * LANES]
                if gain_row is not None:
                    z = _pair_rms(z, gains_ref[gain_row:gain_row + 1, :], lo)
                if use_rope:
                    z = _pair_rope(z, cos_ref[rows, :], sin_ref[rows, :], even_blk)
                if mul != 1.0:
                    z = z * mul
                for half, zh in enumerate((z[:, :HEAD_DIM], z[:, HEAD_DIM:])):
                    if per_request:
                        o_ref[sub, 2 * p + half] = zh.astype(dtype)
                    else:
                        o_ref[2 * p + half, rows, :] = zh.astype(dtype)

        heads_out(qa_o, OFF_QA, NA_HEADS // 2, 0, False, scale, BF16, False)
        heads_out(ka_o, OFF_KA, NA_HEADS // 2, 1, False, 1.0, F32, kv_per_request)
        heads_out(va_o, OFF_VA, NA_HEADS // 2, None, False, 1.0, F32, kv_per_request)
        xr_o[rows, :] = y[:, OFF_XR:OFF_XR + LRU_WIDTH]
        gr_o[rows, :] = y[:, OFF_GR:OFF_GR + LRU_WIDTH]
        heads_out(qc_o, OFF_QC, GQA_HEADS // 2, 2, rope, scale, BF16, False)
        heads_out(kc_o, OFF_KC, GQA_KV_HEADS // 2, 3, rope, 1.0, F32, kv_per_request)
        heads_out(vc_o, OFF_VC, GQA_KV_HEADS // 2, None, False, 1.0, F32, kv_per_request)


def _inproj(x, prev, mod, g, w_bf, gains, rope_tabs, t_req, kv_per_request):
    t, d = x.shape
    tm = INPROJ_TILE
    nsub = tm // TOKEN_SUBTILE
    n_req = t // t_req
    tok = pl.BlockSpec((tm, d), lambda i: (i, 0))
    modspec = _mod_spec(mod, max(t_req // tm, 1))
    in_specs = [tok, modspec, _full(g), _full(w_bf), _full(gains)]
    args = [x, mod, g, w_bf, gains]
    rope = rope_tabs is not None
    if rope:
        tpr = t_req // tm
        in_specs += [pl.BlockSpec((tm, LANES), lambda i: (i % tpr, 0))] * 2
        args += list(rope_tabs)
    has_prev = prev is not None
    if has_prev:
        ffn, mod_prev = prev
        in_specs = [tok, modspec] + in_specs
        args = [ffn, mod_prev] + args

    def hm(nh, dtype):
        return (pl.BlockSpec((nh, tm, HEAD_DIM), lambda i: (0, i, 0)),
                jax.ShapeDtypeStruct((nh, t, HEAD_DIM), dtype))

    def kv(nh):
        if not kv_per_request:
            return hm(nh, F32)
        assert t_req == TOKEN_SUBTILE
        return (pl.BlockSpec((nsub, nh, t_req, HEAD_DIM), lambda i: (i, 0, 0, 0)),
                jax.ShapeDtypeStruct((n_req, nh, t_req, HEAD_DIM), F32))

    def tk(width):
        return (pl.BlockSpec((tm, width), lambda i: (i, 0)), jax.ShapeDtypeStruct((t, width), F32))

    outs = [hm(NA_HEADS, BF16), kv(NA_HEADS), kv(NA_HEADS), tk(LRU_WIDTH), tk(LRU_WIDTH),
            hm(GQA_HEADS, BF16), kv(GQA_KV_HEADS), kv(GQA_KV_HEADS)]
    if has_prev:
        outs = [tk(d)] + outs
    res = pl.pallas_call(
        functools.partial(_inproj_kernel, has_prev, rope, kv_per_request),
        grid=(t // tm,),
        in_specs=in_specs,
        out_specs=[o[0] for o in outs],
        out_shape=[o[1] for o in outs],
        compiler_params=_cparams("arbitrary"),
        name="inproj",
    )(*args)
    if has_prev:
        return res[0], res[1:]
    return x, res


def _softmax_pv(parts):
    m = None
    for s, _ in parts:
        mi = jnp.max(s, axis=-1, keepdims=True)
        m = mi if m is None else jnp.maximum(m, mi)
    den = None
    acc = None
    for s, v in parts:
        p = jnp.exp(s - m)
        di = jnp.sum(p, axis=-1, keepdims=True)
        oi = _dot(p.astype(BF16), v)
        den = di if den is None else den + di
        acc = oi if acc is None else acc + oi
    return acc / den


def _ctx_attn_kernel(qa_ref, ka_ref, va_ref, qc_ref, kc_ref, vc_ref, oa_ref, oc_ref):
    for h in range(NA_HEADS):
        k = ka_ref[h].astype(BF16)
        v = va_ref[h].astype(BF16)
        oa_ref[h] = _softmax_pv([(_dot_nt(qa_ref[h], k), v)]).astype(BF16)
    for h in range(GQA_HEADS):
        j = h // GQA_GROUP
        k = kc_ref[j].astype(BF16)
        v = vc_ref[j].astype(BF16)
        oc_ref[h] = _softmax_pv([(_dot_nt(qc_ref[h], k), v)]).astype(BF16)


def _ctx_attention(qa, ka, va, qc, kc, vc):
    n_req, _, s, _ = ka.shape
    t = qa.shape[1]

    def hm(nh):
        return pl.BlockSpec((nh, s, HEAD_DIM), lambda b: (0, b, 0))

    def pr(nh):
        return pl.BlockSpec((None, nh, s, HEAD_DIM), lambda b: (b, 0, 0, 0))

    return pl.pallas_call(
        _ctx_attn_kernel,
        grid=(n_req,),
        in_specs=[hm(NA_HEADS), pr(NA_HEADS), pr(NA_HEADS), hm(GQA_HEADS), pr(GQA_KV_HEADS), pr(GQA_KV_HEADS)],
        out_specs=[hm(NA_HEADS), hm(GQA_HEADS)],
        out_shape=[jax.ShapeDtypeStruct((NA_HEADS, t, HEAD_DIM), BF16),
                   jax.ShapeDtypeStruct((GQA_HEADS, t, HEAD_DIM), BF16)],
        compiler_params=_cparams("arbitrary"),
        name="ctx_attention",
    )(qa, ka, va, qc, kc, vc)


def _na_lat_kernel(grid_rows, q_ref, k_ref, v_ref, ck_ref, cv_ref, *rest):
    bias_refs, (o_ref, s0, s1) = rest[:NA_BLOCKS_PER_STEP], rest[NA_BLOCKS_PER_STEP:]
    nq = NA_ROW_BLOCK * GRID_W
    nkeys = NA_KEY_ROWS * GRID_W
    sub = nq // NA_Q_SPLIT
    ck = ck_ref[...].astype(BF16)
    cv = cv_ref[...].astype(BF16)
    bufs = (s0, s1)
    chains = [(blk, part) for blk in range(NA_BLOCKS_PER_STEP) for part in range(NA_Q_SPLIT)]

    def window(blk):
        rb = pl.program_id(2) * NA_BLOCKS_PER_STEP + blk
        w0 = jnp.clip(NA_ROW_BLOCK * rb - NA_WIN_H // 2, 0, grid_rows - NA_KEY_ROWS)
        return pl.ds(pl.multiple_of(w0 * GRID_W, 4 * GRID_W), nkeys)

    def scores(n):
        blk, part = chains[n]
        rows = pl.ds(blk * nq + part * sub, sub)
        q = q_ref[rows, :]
        bias_ref = bias_refs[blk]
        buf = bufs[n % 2]
        buf[:, :nkeys] = _dot_nt(q, k_ref[window(blk), :].astype(BF16)) + bias_ref[pl.ds(part * sub, sub), :]
        buf[:, nkeys:] = _dot_nt(q, ck)

    def finish(n):
        blk, part = chains[n]
        s = bufs[n % 2][...]
        m = jnp.max(s, axis=-1, keepdims=True)
        p = jnp.exp(s - m)
        den = jnp.sum(p, axis=-1, keepdims=True)
        pb = p.astype(BF16)
        o = _dot(pb[:, :nkeys], v_ref[window(blk), :].astype(BF16)) + _dot(pb[:, nkeys:], cv)
        o_ref[pl.ds(blk * nq + part * sub, sub), :] = (o / den).astype(BF16)

    scores(0)
    for n in range(len(chains)):
        if n + 1 < len(chains):
            scores(n + 1)
        finish(n)


def _na_bias_table(rpb, grid_rows):
    nrb = grid_rows // NA_ROW_BLOCK
    nh = rpb.shape[0]
    pad = GRID_W - NA_WIN_W
    rp = jnp.pad(rpb.astype(F32), ((0, 0), (0, 0), (pad, pad)))
    toep = jnp.stack([rp[:, :, GRID_W - 1 - c: 2 * GRID_W - 1 - c] for c in range(GRID_W)], axis=2)
    c = jnp.arange(GRID_W)[:, None]
    kc = jnp.arange(GRID_W)[None, :]
    cs = jnp.clip(c - NA_WIN_W // 2, 0, GRID_W - NA_WIN_W)
    toep = jnp.where((kc >= cs) & (kc < cs + NA_WIN_W), toep, MASK_VALUE)
    masked = jnp.full((nh, GRID_W, GRID_W), MASK_VALUE, F32)
    tabs = []
    for rb in (0, min(1, nrb - 1), nrb - 1):
        r0 = NA_ROW_BLOCK * rb
        w0 = min(max(r0 - NA_WIN_H // 2, 0), grid_rows - NA_KEY_ROWS)
        rows = []
        for r in range(r0, r0 + NA_ROW_BLOCK):
            rs = min(max(r - NA_WIN_H // 2, 0), grid_rows - NA_WIN_H)
            blocks = [toep[:, kr - r + NA_WIN_H - 1] if rs <= kr < rs + NA_WIN_H else masked
                      for kr in range(w0, w0 + NA_KEY_ROWS)]
            rows.append(jnp.concatenate(blocks, axis=-1))
        tabs.append(jnp.concatenate(rows, axis=1))
    return jnp.stack(tabs, axis=1)


def _na_lat_attention(qa, ka, va, cache_k, cache_v, layer, bias, dec_seq):
    t = qa.shape[1]
    dec_batch = t // dec_seq
    grid_rows = dec_seq // GRID_W
    nrb = grid_rows // NA_ROW_BLOCK
    nq = NA_ROW_BLOCK * GRID_W

    def variant(rb):
        return jnp.where(rb == 0, 0, jnp.where(rb == nrb - 1, 2, 1))

    nbs = NA_BLOCKS_PER_STEP
    assert nrb % nbs == 0
    nsteps = nrb // nbs
    qspec = pl.BlockSpec((None, nbs * nq, HEAD_DIM), lambda h, b, i: (h, b * nsteps + i, 0))
    kvspec = pl.BlockSpec((None, dec_seq, HEAD_DIM), lambda h, b, i: (h, b, 0))
    cspec = pl.BlockSpec((None, None, None) + cache_k.shape[3:], lambda h, b, i: (b, layer, h, 0, 0))
    bias_specs = [pl.BlockSpec((None, None) + bias.shape[2:],
                               lambda h, b, i, blk=blk: (h, variant(i * nbs + blk), 0, 0)) for blk in range(nbs)]
    nkeys = NA_KEY_ROWS * GRID_W + cache_k.shape[3]
    return pl.pallas_call(
        functools.partial(_na_lat_kernel, grid_rows),
        grid=(NA_HEADS, dec_batch, nsteps),
        in_specs=[qspec, kvspec, kvspec, cspec, cspec] + bias_specs,
        out_specs=qspec,
        out_shape=jax.ShapeDtypeStruct((NA_HEADS, t, HEAD_DIM), BF16),
        scratch_shapes=[pltpu.VMEM((nq // NA_Q_SPLIT, nkeys), F32)] * 2,
        compiler_params=_cparams("arbitrary", "arbitrary", "arbitrary"),
        name="na_lat_attention",
    )(qa, ka, va, cache_k, cache_v, *([bias] * nbs))


def _gqa_lat_kernel(q_ref, k_ref, v_ref, ck_ref, cv_ref, o_ref, *scratch):
    _, tq, hd = q_ref.shape
    g = GQA_GROUP
    nh = GQA_KV_HEADS
    ch = GQA_KEY_CHUNK
    nc = k_ref.shape[1] // ch
    s_bufs = [scratch[2 * j:2 * j + 2] for j in range(nh)]
    qs = [q_ref[j * g:(j + 1) * g].reshape(g * tq, hd) for j in range(nh)]

    def scores(j, c):
        return _dot_nt(qs[j], k_ref[j, pl.ds(c * ch, ch), :].astype(BF16))

    def update(j, state, s, v):
        m, l, acc = state
        m_new = jnp.maximum(m, jnp.max(s, axis=-1, keepdims=True))
        alpha = jnp.exp(m - m_new)
        p = jnp.exp(s - m_new)
        l = alpha * l + jnp.sum(p, axis=-1, keepdims=True)
        acc = alpha * acc + _dot(p.astype(BF16), v)
        return m_new, l, acc

    states = []
    for j in range(nh):
        s = _dot_nt(qs[j], ck_ref[j].astype(BF16))
        m = jnp.max(s, axis=-1, keepdims=True)
        p = jnp.exp(s - m)
        states.append((m, jnp.sum(p, axis=-1, keepdims=True), _dot(p.astype(BF16), cv_ref[j].astype(BF16))))
        s_bufs[j][0][...] = scores(j, 0)

    for c in range(nc):
        for j in range(nh):
            if c + 1 < nc:
                s_bufs[j][(c + 1) % 2][...] = scores(j, c + 1)
            v = v_ref[j, pl.ds(c * ch, ch), :].astype(BF16)
            states[j] = update(j, states[j], s_bufs[j][c % 2][...], v)
    for j in range(nh):
        m, l, acc = states[j]
        o_ref[j * g:(j + 1) * g] = (acc / l).reshape(g, tq, hd).astype(BF16)


def _gqa_lat_attention(qc, kc, vc, cache_k, cache_v, layer, dec_seq):
    t = qc.shape[1]
    dec_batch = t // dec_seq
    tq = GQA_Q_TILE
    nq = dec_seq // tq
    qspec = pl.BlockSpec((GQA_HEADS, tq, HEAD_DIM), lambda b, i: (0, b * nq + i, 0))
    kvspec = pl.BlockSpec((GQA_KV_HEADS, dec_seq, HEAD_DIM), lambda b, i: (0, b, 0))
    cspec = pl.BlockSpec((None, None) + cache_k.shape[2:], lambda b, i: (b, layer, 0, 0, 0))
    return pl.pallas_call(
        _gqa_lat_kernel,
        grid=(dec_batch, nq),
        in_specs=[qspec, kvspec, kvspec, cspec, cspec],
        out_specs=qspec,
        out_shape=jax.ShapeDtypeStruct((GQA_HEADS, t, HEAD_DIM), BF16),
        scratch_shapes=[pltpu.VMEM((GQA_GROUP * tq, GQA_KEY_CHUNK), F32)] * (2 * GQA_KV_HEADS),
        compiler_params=_cparams("arbitrary", "arbitrary"),
        name="gqa_lat_attention",
    )(qc, kc, vc, cache_k, cache_v)


LRU_CHUNK = 256
LRU_PAD = SUBLANES


def _lru_kernel(xr_ref, gr_ref, h0_ref, cw_ref, cb_ref, w_ref, b_ref, lam_ref,
                y_ref, st_ref, xpad, a_f, u_f, a_b, u_b):
    t, w = xr_ref.shape
    nch = t // LRU_CHUNK
    ngrp = t // SUBLANES

    zeros = jnp.zeros((LRU_PAD, w), F32)
    xpad[pl.ds(0, LRU_PAD), :] = zeros
    xpad[pl.ds(LRU_PAD + t, LRU_PAD), :] = zeros
    xpad[pl.ds(LRU_PAD, t), :] = xr_ref[...]

    lam = lam_ref[...]
    decay = -LRU_C * (jnp.maximum(-lam, 0.0) + jnp.log1p(jnp.exp(-jnp.abs(lam))))
    h0 = h0_ref[...]
    row = lax.broadcasted_iota(I32, (LRU_CHUNK, 1), 0)
    srow = lax.broadcasted_iota(I32, (SUBLANES, 1), 0)
    left = CONV_W // 2

    def gates(c, _):
        t0 = pl.multiple_of(c * LRU_CHUNK, LRU_CHUNK)
        cur = xpad[pl.ds(t0 + LRU_PAD, LRU_CHUNK), :]
        before = xpad[pl.ds(t0, LRU_PAD), :]
        after = xpad[pl.ds(t0 + LRU_PAD + LRU_CHUNK, LRU_PAD), :]
        xc = cb_ref[...]
        for j in range(CONV_W):
            s = left - j
            if s > 0:
                sh = pltpu.roll(cur, s, 0)
                head = jnp.where(srow < s, pltpu.roll(before, s, 0), sh[:SUBLANES])
                tap = jnp.concatenate([head, sh[SUBLANES:]], axis=0)
            elif s < 0:
                sh = pltpu.roll(cur, LRU_CHUNK + s, 0)
                tail = jnp.where(srow >= SUBLANES + s, pltpu.roll(after, SUBLANES + s, 0), sh[-SUBLANES:])
                tap = jnp.concatenate([sh[:-SUBLANES], tail], axis=0)
            else:
                tap = cur
            xc = xc + tap * cw_ref[j:j + 1, :]
        z = _dot(xc.astype(BF16), w_ref[...]) + b_ref[...]
        for d, (a_ref, u_ref, edge_chunk, edge_row) in enumerate(
                ((a_f, u_f, 0, 0), (a_b, u_b, nch - 1, LRU_CHUNK - 1))):
            r = jax.nn.sigmoid(z[:, (2 * d) * w:(2 * d + 1) * w])
            i = jax.nn.sigmoid(z[:, (2 * d + 1) * w:(2 * d + 2) * w])
            log_a = decay[d:d + 1, :] * r
            a = jnp.exp(log_a)
            u = jnp.sqrt(-jnp.tanh(log_a) * (a * a + 1.0)) * (i * xc)
            first = (row == edge_row) & (c == edge_chunk)
            u = u + jnp.where(first, a * h0[d:d + 1, :], 0.0)
            a_ref[pl.ds(t0, LRU_CHUNK), :] = a
            u_ref[pl.ds(t0, LRU_CHUNK), :] = u
        return 0

    lax.fori_loop(0, nch, gates, 0)

    def scan(i, carry):
        hf, hb = carry
        o = pl.multiple_of(i * SUBLANES, SUBLANES)
        a = a_f[pl.ds(o, SUBLANES), :]
        u = u_f[pl.ds(o, SUBLANES), :]
        for s in (1, 2, 4):
            keep = srow >= s
            a_s = jnp.where(keep, pltpu.roll(a, s, 0), 1.0)
            u_s = jnp.where(keep, pltpu.roll(u, s, 0), 0.0)
            u = a * u_s + u
            a = a * a_s
        h = u + a * hf
        u_f[pl.ds(o, SUBLANES), :] = h
        hf = h[SUBLANES - 1:SUBLANES, :]
        o = pl.multiple_of((ngrp - 1 - i) * SUBLANES, SUBLANES)
        a = a_b[pl.ds(o, SUBLANES), :]
        u = u_b[pl.ds(o, SUBLANES), :]
        for s in (1, 2, 4):
            keep = srow < SUBLANES - s
            a_s = jnp.where(keep, pltpu.roll(a, SUBLANES - s, 0), 1.0)
            u_s = jnp.where(keep, pltpu.roll(u, SUBLANES - s, 0), 0.0)
            u = a * u_s + u
            a = a * a_s
        h = u + a * hb
        u_b[pl.ds(o, SUBLANES), :] = h
        hb = h[0:1, :]
        return hf, hb

    zero_row = jnp.zeros((1, w), F32)
    hf, hb = lax.fori_loop(0, ngrp, scan, (zero_row, zero_row))
    st_ref[0:1, :] = hf
    st_ref[1:2, :] = hb

    def emit(c, _):
        t0 = pl.multiple_of(c * LRU_CHUNK, LRU_CHUNK)
        hs = u_f[pl.ds(t0, LRU_CHUNK), :] + u_b[pl.ds(t0, LRU_CHUNK), :]
        y = jax.nn.gelu(gr_ref[pl.ds(t0, LRU_CHUNK), :]) * hs
        y_ref[pl.ds(t0, LRU_CHUNK), :] = y.astype(y_ref.dtype)
        return 0

    lax.fori_loop(0, nch, emit, 0)


def _lru(xr, gr, h0, cw, cb, w_bf, b_cat, lam, t_req):
    t, w = xr.shape
    n_req = t // t_req
    tok = pl.BlockSpec((t_req, w), lambda b: (b, 0))
    st = pl.BlockSpec((None, 2, w), lambda b: (b, 0, 0))
    return pl.pallas_call(
        _lru_kernel,
        grid=(n_req,),
        in_specs=[tok, tok, st, _full(cw), _full(cb), _full(w_bf), _full(b_cat), _full(lam)],
        out_specs=[tok, st],
        out_shape=[jax.ShapeDtypeStruct((t, w), BF16), jax.ShapeDtypeStruct((n_req, 2, w), F32)],
        scratch_shapes=[pltpu.VMEM((t_req + 2 * LRU_PAD, w), F32)] + [pltpu.VMEM((t_req, w), F32)] * 4,
        compiler_params=_cparams("arbitrary"),
        name="rglru",
    )(xr, gr, h0, cw, cb, w_bf, b_cat, lam)


def _outproj_kernel(x_ref, oa_ref, ob_ref, oc_ref, mod_ref, w_ref, gf_ref, rw_ref,
                    xm_ref, u_ref, aff_ref):
    d = x_ref.shape[-1]
    g1 = mod_ref[:, 2 * d:3 * d]
    sh2 = mod_ref[:, 3 * d:4 * d]
    sc2 = mod_ref[:, 4 * d:5 * d]
    rw = rw_ref[...]
    r_hi = rw.astype(BF16)
    r_lo = (rw - r_hi.astype(F32)).astype(BF16)
    for sub in range(x_ref.shape[0] // TOKEN_SUBTILE):
        rows = pl.ds(sub * TOKEN_SUBTILE, TOKEN_SUBTILE)
        pieces = ([oa_ref[h, rows, :].astype(F32) for h in range(NA_HEADS)] + [ob_ref[rows, :].astype(F32)]
                  + [oc_ref[h, rows, :].astype(F32) for h in range(GQA_HEADS)])
        o = jnp.concatenate(pieces, axis=-1).astype(BF16)
        mix = _dot(o, w_ref[...])
        xm = x_ref[rows, :] + g1 * mix
        xm_ref[rows, :] = xm
        u = _rms(xm, gf_ref[...]) * (1.0 + sc2) + sh2
        u_ref[rows, :] = u
        u_hi = u.astype(BF16)
        u_lo = (u - u_hi.astype(F32)).astype(BF16)
        lg = _dot_nt(r_hi, u_hi) + (_dot_nt(r_lo, u_hi) + _dot_nt(r_hi, u_lo))
        m = jnp.max(lg, axis=0, keepdims=True)
        e = jnp.exp(lg - m)
        aff_ref[:, rows] = e / jnp.sum(e, axis=0, keepdims=True)


def _outproj(x, oa, ob, oc, mod, w_bf, gf, rw_t, t_req):
    t, d = x.shape
    tm = TOKEN_TILE
    tpr = max(t_req // tm, 1)
    tok = pl.BlockSpec((tm, d), lambda i: (i, 0))
    hm = pl.BlockSpec((NA_HEADS, tm, HEAD_DIM), lambda i: (0, i, 0))
    return pl.pallas_call(
        _outproj_kernel,
        grid=(t // tm,),
        in_specs=[tok, hm, pl.BlockSpec((tm, LRU_WIDTH), lambda i: (i, 0)), hm,
                  _mod_spec(mod, tpr), _full(w_bf), _full(gf), _full(rw_t)],
        out_specs=[tok, tok, pl.BlockSpec((N_EXPERTS, tm), lambda i: (0, i))],
        out_shape=[jax.ShapeDtypeStruct((t, d), F32), jax.ShapeDtypeStruct((t, d), F32),
                   jax.ShapeDtypeStruct((N_EXPERTS, t), F32)],
        compiler_params=_cparams("arbitrary"),
        name="outproj_router",
    )(x, oa, ob, oc, mod, w_bf, gf, rw_t)


PREFIX_BLOCK = 256
TOPK_REQUESTS_PER_STEP = 8


def _topk_kernel(cap, t_req, compact, aff_ref, *outs):
    ne, tt = aff_ref.shape
    ng = tt // t_req
    slot_ref = outs[0]
    keys = [pltpu.bitcast(aff_ref[:, g * t_req:(g + 1) * t_req], I32) for g in range(ng)]

    def search(it, thrs):
        bit = jnp.left_shift(jnp.int32(1), 30 - it)
        out = []
        for g in range(ng):
            cand = thrs[g] | bit
            cnt = jnp.sum((keys[g] >= cand).astype(F32), axis=-1, keepdims=True)
            out.append(jnp.where(cnt >= cap, cand, thrs[g]))
        return tuple(out)

    thrs = lax.fori_loop(0, 31, search, tuple(jnp.zeros((ne, 1), I32) for _ in range(ng)))

    ri = lax.broadcasted_iota(I32, (PREFIX_BLOCK, PREFIX_BLOCK), 0)
    ci = lax.broadcasted_iota(I32, (PREFIX_BLOCK, PREFIX_BLOCK), 1)
    tri = jnp.where(ri <= ci, 1.0, 0.0).astype(BF16)

    def prefix(mask_f32):
        carry = jnp.zeros((ne, 1), F32)
        blocks = []
        for b in range(t_req // PREFIX_BLOCK):
            blk = mask_f32[:, b * PREFIX_BLOCK:(b + 1) * PREFIX_BLOCK]
            blocks.append(_dot(blk.astype(BF16), tri) + carry)
            carry = carry + jnp.sum(blk, axis=-1, keepdims=True)
        return blocks[0] if len(blocks) == 1 else jnp.concatenate(blocks, axis=-1)

    for g in range(ng):
        gt = keys[g] > thrs[g]
        eq = keys[g] == thrs[g]
        need = cap - jnp.sum(gt.astype(F32), axis=-1, keepdims=True)
        tie_rank = prefix(eq.astype(F32))
        sel = jnp.where(gt, 1.0, jnp.where(eq, (tie_rank <= need).astype(F32), 0.0))
        slot = jnp.where(sel > 0.0, prefix(sel), 0.0)
        slot_ref[:, g * t_req:(g + 1) * t_req] = slot

    if not compact:
        return
    assert ng == 1
    idx_ref, gate_ref = outs[1], outs[2]
    tok = lax.broadcasted_iota(I32, (ne, t_req), 1)
    valid = (slot > 0.0).astype(I32)
    dist = jnp.where(slot > 0.0, tok - (slot.astype(I32) - 1), 0)
    tokv = tok
    gate = aff_ref[...]
    for k in range((t_req - 1).bit_length()):
        sh = 1 << k
        nb = lambda x: pltpu.roll(x, t_req - sh, 1)
        n_valid, n_dist = nb(valid), nb(dist)
        incoming = (n_valid * ((n_dist >> k) & 1)) > 0
        stay = valid * (1 - ((dist >> k) & 1))
        tokv = jnp.where(incoming, nb(tokv), tokv)
        gate = jnp.where(incoming, nb(gate), gate)
        dist = jnp.where(incoming, n_dist, dist)
        valid = jnp.where(incoming, 1, stay)
    idx_ref[...] = tokv[:, :cap]
    gate_ref[...] = gate[:, :cap]


def _topk(aff, t_req, cap, compact):
    ne, t = aff.shape
    n_req = t // t_req
    ng = 1 if compact else min(TOPK_REQUESTS_PER_STEP, n_req)
    assert n_req % ng == 0
    lanes = ng * t_req
    out_specs = [pl.BlockSpec((ne, lanes), lambda b: (0, b))]
    out_shape = [jax.ShapeDtypeStruct((ne, t), F32)]
    if compact:
        out_specs += [pl.BlockSpec((None, ne, cap), lambda b: (b, 0, 0))] * 2
        out_shape += [jax.ShapeDtypeStruct((n_req, ne, cap), I32), jax.ShapeDtypeStruct((n_req, ne, cap), F32)]
    return pl.pallas_call(
        functools.partial(_topk_kernel, cap, t_req, compact),
        grid=(n_req // ng,),
        in_specs=[pl.BlockSpec((ne, lanes), lambda b: (0, b))],
        out_specs=out_specs,
        out_shape=out_shape,
        compiler_params=_cparams("arbitrary"),
        name="expert_topk",
    )(aff)


def _slot_onehot(slot_ref, cap):
    ne = slot_ref.shape[0]
    want = (lax.broadcasted_iota(I32, (cap, 1), 0) + 1).astype(F32)
    return jnp.concatenate([jnp.where(slot_ref[e:e + 1, :] == want, 1.0, 0.0) for e in range(ne)], axis=0)


def _ctx_dispatch_kernel(slot_ref, u_ref, x_ref):
    ne, cap, d = x_ref.shape
    onehot = _slot_onehot(slot_ref, cap).astype(BF16)
    x = _dot(onehot, u_ref[...].astype(BF16))
    x_ref[...] = x.reshape(ne, cap, d).astype(BF16)


def _ctx_dispatch(slot, u, t_req, cap):
    ne, t = slot.shape
    n_req = t // t_req
    d = u.shape[-1]
    return pl.pallas_call(
        _ctx_dispatch_kernel,
        grid=(n_req,),
        in_specs=[pl.BlockSpec((ne, t_req), lambda b: (0, b)),
                  pl.BlockSpec((t_req, d), lambda b: (b, 0))],
        out_specs=pl.BlockSpec((ne, None, cap, d), lambda b: (0, b, 0, 0)),
        out_shape=jax.ShapeDtypeStruct((ne, n_req, cap, d), BF16),
        compiler_params=_cparams("arbitrary"),
        name="ctx_dispatch",
    )(slot, u)


def _ctx_combine_kernel(slot_ref, aff_ref, y_ref, o_ref):
    ne, cap, d = y_ref.shape
    want = (lax.broadcasted_iota(I32, (cap, 1), 0) + 1).astype(F32)
    hots, gates = [], []
    for e in range(ne):
        hot = jnp.where(slot_ref[e:e + 1, :] == want, 1.0, 0.0)
        hots.append(hot)
        gates.append(jnp.sum(hot * aff_ref[e:e + 1, :], axis=-1, keepdims=True))
    onehot = jnp.concatenate(hots, axis=0).astype(BF16)
    y = y_ref[...].reshape(ne * cap, d) * jnp.concatenate(gates, axis=0)
    y_hi = y.astype(BF16)
    y_lo = (y - y_hi.astype(F32)).astype(BF16)
    tn = (((0,), (0,)), ((), ()))
    o_ref[...] = (lax.dot_general(onehot, y_hi, tn, preferred_element_type=F32)
                  + lax.dot_general(onehot, y_lo, tn, preferred_element_type=F32))


def _ctx_combine(slot, aff, ye, t_req):
    ne, n_req, cap, d = ye.shape
    lane = pl.BlockSpec((ne, t_req), lambda b: (0, b))
    return pl.pallas_call(
        _ctx_combine_kernel,
        grid=(n_req,),
        in_specs=[lane, lane, pl.BlockSpec((ne, None, cap, d), lambda b: (0, b, 0, 0))],
        out_specs=pl.BlockSpec((t_req, d), lambda b: (b, 0)),
        out_shape=jax.ShapeDtypeStruct((n_req * t_req, d), F32),
        compiler_params=_cparams("arbitrary"),
        name="ctx_combine",
    )(slot, aff, ye)


SC_GATHER_ROWS = 64


def _lat_dispatch(gidx, u):
    ne, rows = gidx.shape
    d = u.shape[-1]
    total = ne * rows
    mesh = plsc.VectorSubcoreMesh(core_axis_name="c", subcore_axis_name="s")
    n_cores = mesh.num_cores
    n_workers = n_cores * mesh.num_subcores
    chunk = SC_GATHER_ROWS
    per_worker = total // n_workers
    assert total % n_workers == 0 and per_worker % chunk == 0

    @functools.partial(
        pl.kernel, mesh=mesh, out_type=jax.ShapeDtypeStruct((total, d), u.dtype),
        scratch_types=[pltpu.VMEM((chunk,), I32), pltpu.VMEM((chunk, d), u.dtype), pltpu.SemaphoreType.DMA])
    def gather(u_hbm, idx_hbm, out_hbm, idx_v, rows_v, sem):
        worker = lax.axis_index("s") * n_cores + lax.axis_index("c")
        base = worker * per_worker

        @pl.loop(0, per_worker // chunk)
        def _(i):
            off = pl.multiple_of(base + i * chunk, SUBLANES)
            pltpu.sync_copy(idx_hbm.at[pl.ds(off, chunk)], idx_v)
            pltpu.async_copy(u_hbm.at[idx_v], rows_v, sem).wait()
            pltpu.sync_copy(rows_v, out_hbm.at[pl.ds(off, chunk)])

    return gather(u, gidx.reshape(total)).reshape(ne, rows, d)


def _ffn_kernel(nt_ctx, xc_ref, xl_ref, wg_ref, wu_ref, wd_ref, yc_ref, yl_ref, wg_bf, wu_bf, wd_bf):
    j = pl.program_id(1)

    @pl.when(j == 0)
    def _():
        wg_bf[...] = wg_ref[...].astype(BF16)
        wu_bf[...] = wu_ref[...].astype(BF16)
        wd_bf[...] = wd_ref[...].astype(BF16)

    def ffn(x):
        h = jax.nn.silu(_dot(x, wg_bf[...])) * _dot(x, wu_bf[...])
        return _dot(h.astype(BF16), wd_bf[...])

    @pl.when(j < nt_ctx)
    def _():
        yc_ref[...] = ffn(xc_ref[...])

    @pl.when(j >= nt_ctx)
    def _():
        yl_ref[...] = ffn(xl_ref[...].astype(BF16))


def _expert_ffn(x_ctx, x_lat, wg, wu, wd, layer):
    ne, rows_ctx, d = x_ctx.shape
    rows_lat = x_lat.shape[1]
    ff = wg.shape[-1]
    tr = FFN_ROW_TILE
    nt_ctx, nt_lat = rows_ctx // tr, rows_lat // tr
    wspec = lambda a: pl.BlockSpec((None, None) + a.shape[2:], lambda e, j: (layer, e, 0, 0))
    ctx_spec = pl.BlockSpec((None, tr, d), lambda e, j: (e, jnp.minimum(j, nt_ctx - 1), 0))
    lat_spec = pl.BlockSpec((None, tr, d), lambda e, j: (e, jnp.maximum(j - nt_ctx, 0), 0))
    return pl.pallas_call(
        functools.partial(_ffn_kernel, nt_ctx),
        grid=(ne, nt_ctx + nt_lat),
        in_specs=[ctx_spec, lat_spec, wspec(wg), wspec(wu), wspec(wd)],
        out_specs=[ctx_spec, lat_spec],
        out_shape=[jax.ShapeDtypeStruct((ne, rows_ctx, d), F32), jax.ShapeDtypeStruct((ne, rows_lat, d), F32)],
        scratch_shapes=[pltpu.VMEM((d, ff), BF16), pltpu.VMEM((d, ff), BF16), pltpu.VMEM((ff, d), BF16)],
        compiler_params=_cparams("arbitrary", "arbitrary"),
        name="expert_ffn",
    )(x_ctx, x_lat, wg, wu, wd)


COMBINE_GROUP = 4


def _lat_combine_kernel(idx_ref, gate_ref, y_ref, o_ref):
    e = pl.program_id(1)
    cap = y_ref.shape[0]

    @pl.when(e == 0)
    def _():
        o_ref[...] = jnp.zeros(o_ref.shape, o_ref.dtype)

    def body(q, _):
        r0 = q * COMBINE_GROUP
        toks = [idx_ref[0, r0 + k] for k in range(COMBINE_GROUP)]
        acc = [o_ref[pl.ds(i, 1), :] for i in toks]
        for k in range(COMBINE_GROUP):
            o_ref[pl.ds(toks[k], 1), :] = acc[k] + gate_ref[0, r0 + k] * y_ref[pl.ds(r0 + k, 1), :]
        return 0

    lax.fori_loop(0, cap // COMBINE_GROUP, body, 0)


def _lat_combine(idx, gate, ye, t_req):
    n_req, ne, _, cap = idx.shape
    d = ye.shape[-1]
    sspec = pl.BlockSpec((None, None, 1, cap), lambda b, e: (b, e, 0, 0), memory_space=pltpu.SMEM)
    return pl.pallas_call(
        _lat_combine_kernel,
        grid=(n_req, ne),
        in_specs=[sspec, sspec, pl.BlockSpec((None, cap, d), lambda b, e: (e, b, 0))],
        out_specs=pl.BlockSpec((t_req, d), lambda b, e: (b, 0)),
        out_shape=jax.ShapeDtypeStruct((n_req * t_req, d), F32),
        compiler_params=_cparams("arbitrary", "arbitrary"),
        name="lat_combine",
    )(idx, gate, ye)


def _final_kernel(xm_ref, f_ref, mod_ref, o_ref):
    d = xm_ref.shape[-1]
    o_ref[...] = xm_ref[...] + mod_ref[:, 5 * d:6 * d] * f_ref[...]


def _final(xm, ffn, mod, t_req):
    t, d = xm.shape
    tm = TOKEN_TILE
    tpr = max(t_req // tm, 1)
    tok = pl.BlockSpec((tm, d), lambda i: (i, 0))
    return pl.pallas_call(
        _final_kernel,
        grid=(t // tm,),
        in_specs=[tok, tok, _mod_spec(mod, tpr)],
        out_specs=tok,
        out_shape=jax.ShapeDtypeStruct((t, d), F32),
        compiler_params=_cparams("arbitrary"),
        name="final_residual",
    )(xm, ffn, mod)


def _rope_tables(dec_seq):
    pos = jnp.arange(dec_seq)
    n = HEAD_DIM // 4
    inv = ROPE_THETA ** (-jnp.arange(n, dtype=F32) / n)
    ang_r = (pos // GRID_W).astype(F32)[:, None] * inv[None, :]
    ang_c = (pos % GRID_W).astype(F32)[:, None] * inv[None, :]
    cr, sr, cc, sc = jnp.cos(ang_r), jnp.sin(ang_r), jnp.cos(ang_c), jnp.sin(ang_c)
    cos = jnp.concatenate([cr, cr, cc, cc] * 2, axis=-1)
    sin = jnp.concatenate([-sr, sr, -sc, sc] * 2, axis=-1)
    return cos, sin


def _block_diag(wts):
    nb, bw, _ = wts.shape
    out = jnp.zeros((nb * bw, nb * bw), wts.dtype)
    for n in range(nb):
        out = out.at[n * bw:(n + 1) * bw, n * bw:(n + 1) * bw].set(wts[n])
    return out


def kernel(x_prompt, x_sample, c, cache_na_k, cache_na_v, cache_gqa_k, cache_gqa_v, state_lru, c_ctx, mod_w, mod_b, norm_mix_g, norm_ffn_g, w_in, na_q_norm_g, na_k_norm_g, na_rpb, conv_w, conv_b, lru_w_a, lru_b_a, lru_w_i, lru_b_i, lru_lambda, gqa_q_norm_g, gqa_k_norm_g, w_out, router_w, expert_w_gate, expert_w_up, expert_w_down):
    batch, seq, d = x_prompt.shape
    dec_batch, dec_seq, _ = x_sample.shape
    depth = mod_w.shape[0]
    t_ctx = batch * seq
    t_lat = dec_batch * dec_seq
    assert seq == TOKEN_SUBTILE and dec_seq % TOKEN_TILE == 0 and (batch * seq) % TOKEN_TILE == 0
    assert (dec_seq // GRID_W) % NA_ROW_BLOCK == 0 and dec_seq // GRID_W >= NA_KEY_ROWS
    cap_ctx = EC_FACTOR * seq // N_EXPERTS
    cap_lat = EC_FACTOR * dec_seq // N_EXPERTS
    rows_ctx = batch * cap_ctx
    rows_lat = dec_batch * cap_lat
    assert rows_ctx % FFN_ROW_TILE == 0 and rows_lat % FFN_ROW_TILE == 0 and cap_lat % COMBINE_GROUP == 0

    n_mod_rows = SUBLANES
    assert 1 + dec_batch <= n_mod_rows
    cvec = jnp.zeros((n_mod_rows, d), F32).at[0].set(c_ctx).at[1:1 + dec_batch].set(c)
    mods = _modulation(cvec, mod_w, mod_b)

    rope_tabs = _rope_tables(dec_seq)
    x_c = x_prompt.reshape(t_ctx, d)
    x_l = x_sample.reshape(t_lat, d)
    zeros_h0 = jnp.zeros((batch, 2, LRU_WIDTH), F32)

    new_na_k, new_na_v, new_gqa_k, new_gqa_v, new_lru = [], [], [], [], []
    prev_c = prev_l = None
    for l in range(depth):
        mod_c = mods[l, 0:1].reshape(1, 1, 6 * d)
        mod_l = mods[l, 1:1 + dec_batch].reshape(dec_batch, 1, 6 * d)
        tile2 = lambda g: jnp.concatenate([g, g])
        gains = jnp.stack([tile2(na_q_norm_g[l]), tile2(na_k_norm_g[l]),
                           tile2(gqa_q_norm_g[l]), tile2(gqa_k_norm_g[l])])
        w_in_bf = w_in[l].astype(BF16)
        g_mix = norm_mix_g[l][None, :]
        x_c, (qa_c, ka_c, va_c, xr_c, gr_c, qc_c, kc_c, vc_c) = _inproj(
            x_c, prev_c, mod_c, g_mix, w_in_bf, gains, None, seq, True)
        x_l, (qa_l, ka_l, va_l, xr_l, gr_l, qc_l, kc_l, vc_l) = _inproj(
            x_l, prev_l, mod_l, g_mix, w_in_bf, gains, rope_tabs, dec_seq, False)

        oa_c, oc_c = _ctx_attention(qa_c, ka_c, va_c, qc_c, kc_c, vc_c)
        bias = _na_bias_table(na_rpb[l], dec_seq // GRID_W)
        oa_l = _na_lat_attention(qa_l, ka_l, va_l, cache_na_k, cache_na_v, l, bias, dec_seq)
        oc_l = _gqa_lat_attention(qc_l, kc_l, vc_l, cache_gqa_k, cache_gqa_v, l, dec_seq)

        w_gate = jnp.concatenate([_block_diag(lru_w_a[l, 0]), _block_diag(lru_w_i[l, 0]),
                                  _block_diag(lru_w_a[l, 1]), _block_diag(lru_w_i[l, 1])], axis=1).astype(BF16)
        b_gate = jnp.concatenate([lru_b_a[l, 0], lru_b_i[l, 0], lru_b_a[l, 1], lru_b_i[l, 1]])[None, :]
        lru_args = (conv_w[l], conv_b[l][None, :], w_gate, b_gate, lru_lambda[l])
        ob_c, st = _lru(xr_c, gr_c, zeros_h0, *lru_args, seq)
        ob_l, _ = _lru(xr_l, gr_l, state_lru[:, l], *lru_args, dec_seq)

        w_out_bf = w_out[l].astype(BF16)
        g_ffn = norm_ffn_g[l][None, :]
        rw_t = router_w[l].T
        xm_c, u_c, aff_c = _outproj(x_c, oa_c, ob_c, oc_c, mod_c, w_out_bf, g_ffn, rw_t, seq)
        xm_l, u_l, aff_l = _outproj(x_l, oa_l, ob_l, oc_l, mod_l, w_out_bf, g_ffn, rw_t, dec_seq)

        (slot_c,) = _topk(aff_c, seq, cap_ctx, False)
        _, idx_l, gate_l = _topk(aff_l, dec_seq, cap_lat, True)
        g_l = idx_l + (jnp.arange(dec_batch, dtype=I32) * dec_seq)[:, None, None]
        gidx_l = g_l.transpose(1, 0, 2).reshape(N_EXPERTS, rows_lat)
        idx_l = idx_l.reshape(dec_batch, N_EXPERTS, 1, cap_lat)
        gate_l = gate_l.reshape(dec_batch, N_EXPERTS, 1, cap_lat)

        xe_c = _ctx_dispatch(slot_c, u_c, seq, cap_ctx).reshape(N_EXPERTS, rows_ctx, d)
        xe_l = _lat_dispatch(gidx_l, u_l)
        ye_c, ye_l = _expert_ffn(xe_c, xe_l, expert_w_gate, expert_w_up, expert_w_down, l)
        ffn_c = _ctx_combine(slot_c, aff_c, ye_c.reshape(N_EXPERTS, batch, cap_ctx, d), seq)
        ffn_l = _lat_combine(idx_l, gate_l, ye_l, dec_seq)
        prev_c, prev_l = (ffn_c, mod_c), (ffn_l, mod_l)
        x_c, x_l = xm_c, xm_l

        new_na_k.append(ka_c)
        new_na_v.append(va_c)
        new_gqa_k.append(kc_c)
        new_gqa_v.append(vc_c)
        new_lru.append(st)

    y_p = _final(x_c, prev_c[0], prev_c[1], seq)
    y_s = _final(x_l, prev_l[0], prev_l[1], dec_seq)
    return (y_p.reshape(batch, seq, d), y_s.reshape(dec_batch, dec_seq, d),
            jnp.stack(new_na_k, axis=1), jnp.stack(new_na_v, axis=1),
            jnp.stack(new_gqa_k, axis=1), jnp.stack(new_gqa_v, axis=1),
            jnp.stack(new_lru, axis=1))
```

```python
import functools

import jax
import jax.numpy as jnp
from jax import lax
from jax.experimental import pallas as pl
from jax.experimental.pallas import tpu as pltpu
from jax.experimental.pallas import tpu_sc as plsc

F32 = jnp.float32
BF16 = jnp.bfloat16
I32 = jnp.int32

HEAD_DIM = 64
NA_HEADS = 6
GQA_HEADS = 6
GQA_KV_HEADS = 2
GQA_GROUP = GQA_HEADS // GQA_KV_HEADS
LRU_WIDTH = 256
LRU_C = 8.0
CONV_W = 4
GRID_W = 64
NA_WIN_H = 8
NA_WIN_W = 16
N_EXPERTS = 16
EC_FACTOR = 2
ROPE_THETA = 10000.0
EPS = 1e-6
NA_DIM = NA_HEADS * HEAD_DIM
GQA_Q_DIM = GQA_HEADS * HEAD_DIM
GQA_KV_DIM = GQA_KV_HEADS * HEAD_DIM
OFF_QA = 0
OFF_KA = OFF_QA + NA_DIM
OFF_VA = OFF_KA + NA_DIM
OFF_XR = OFF_VA + NA_DIM
OFF_GR = OFF_XR + LRU_WIDTH
OFF_QC = OFF_GR + LRU_WIDTH
OFF_KC = OFF_QC + GQA_Q_DIM
OFF_VC = OFF_KC + GQA_KV_DIM

LANES = 128
SUBLANES = 8
VMEM_LIMIT_BYTES = 56 * 1024 * 1024

TOKEN_TILE = 512
TOKEN_SUBTILE = 256
INPROJ_TILE = 256
NA_ROW_BLOCK = 4
NA_KEY_ROWS = NA_ROW_BLOCK + NA_WIN_H
NA_BLOCKS_PER_STEP = 4
NA_Q_SPLIT = 1
GQA_Q_TILE = 256
GQA_KEY_CHUNK = 1024
FFN_ROW_TILE = 512
MASK_VALUE = -1e30


def _cparams(*sem, flags=None):
    return pltpu.CompilerParams(dimension_semantics=sem, vmem_limit_bytes=VMEM_LIMIT_BYTES, flags=flags)


def _dot(a, b):
    return jnp.dot(a, b, preferred_element_type=F32)


def _dot_nt(a, b):
    return lax.dot_general(a, b, (((1,), (1,)), ((), ())), preferred_element_type=F32)


def _rms(x, g):
    ms = jnp.mean(x * x, axis=-1, keepdims=True)
    return x * lax.rsqrt(ms + EPS) * g


def _full(a):
    nd = a.ndim
    return pl.BlockSpec(a.shape, lambda *_: (0,) * nd)


def _mod_spec(mod, tiles_per_request):
    blk = (None, 1, mod.shape[-1])
    if mod.shape[0] == 1:
        return pl.BlockSpec(blk, lambda i: (0, 0, 0))
    return pl.BlockSpec(blk, lambda i: (i // tiles_per_request, 0, 0))


def _mod_kernel(c_ref, w_ref, b_ref, o_ref):
    s = jax.nn.silu(c_ref[...])
    o_ref[...] = jnp.dot(s, w_ref[...], precision=lax.Precision.HIGHEST,
                         preferred_element_type=F32) + b_ref[...]


def _modulation(cvec, mod_w, mod_b):
    depth, d, n = mod_w.shape
    rows = cvec.shape[0]
    tn = 1536
    return pl.pallas_call(
        _mod_kernel,
        grid=(depth, n // tn),
        in_specs=[
            pl.BlockSpec((rows, d), lambda l, j: (0, 0)),
            pl.BlockSpec((None, d, tn), lambda l, j: (l, 0, j)),
            pl.BlockSpec((None, 1, tn), lambda l, j: (l, 0, j)),
        ],
        out_specs=pl.BlockSpec((None, rows, tn), lambda l, j: (l, 0, j)),
        out_shape=jax.ShapeDtypeStruct((depth, rows, n), F32),
        compiler_params=_cparams("arbitrary", "arbitrary"),
        name="modulation",
    )(cvec, mod_w, mod_b.reshape(depth, 1, n))


def _pair_rms(z, gain, lo):
    z2 = z * z
    s_lo = jnp.sum(jnp.where(lo, z2, 0.0), axis=-1, keepdims=True)
    s_hi = jnp.sum(jnp.where(lo, 0.0, z2), axis=-1, keepdims=True)
    r = jnp.where(lo, lax.rsqrt(s_lo * (1.0 / HEAD_DIM) + EPS), lax.rsqrt(s_hi * (1.0 / HEAD_DIM) + EPS))
    return z * r * gain


def _pair_rope(z, cos, sin_signed, even_blk):
    nxt = pltpu.roll(z, LANES - 16, 1)
    prv = pltpu.roll(z, 16, 1)
    return z * cos + jnp.where(even_blk, nxt, prv) * sin_signed


def _inproj_kernel(has_prev, rope, kv_per_request, *refs):
    if has_prev:
        f_ref, mprev_ref, refs = refs[0], refs[1], refs[2:]
    x_ref, mod_ref, g_ref, w_ref, gains_ref = refs[:5]
    refs = refs[5:]
    if rope:
        cos_ref, sin_ref, refs = refs[0], refs[1], refs[2:]
    if has_prev:
        x_out, refs = refs[0], refs[1:]
    qa_o, ka_o, va_o, xr_o, gr_o, qc_o, kc_o, vc_o = refs
    d = x_ref.shape[-1]
    lane = lax.broadcasted_iota(I32, (1, LANES), 1)
    lo = lane < HEAD_DIM
    even_blk = (lane // 16) % 2 == 0
    scale = HEAD_DIM ** -0.5
    sh1 = mod_ref[:, 0:d]
    sc1 = mod_ref[:, d:2 * d]

    for sub in range(x_ref.shape[0] // TOKEN_SUBTILE):
        rows = pl.ds(sub * TOKEN_SUBTILE, TOKEN_SUBTILE)
        x = x_ref[rows, :]
        if has_prev:
            x = x + mprev_ref[:, 5 * d:6 * d] * f_ref[rows, :]
            x_out[rows, :] = x
        u = _rms(x, g_ref[...]) * (1.0 + sc1) + sh1
        y = _dot(u.astype(BF16), w_ref[...])

        def heads_out(o_ref, off, n_pairs, gain_row, use_rope, mul, dtype, per_request):
            for p in range(n_pairs):
                z = y[:, off + p * LANES: off + (p + 1) * LANES]
                if gain_row is not None:
                    z = _pair_rms(z, gains_ref[gain_row:gain_row + 1, :], lo)
                if use_rope:
                    z = _pair_rope(z, cos_ref[rows, :], sin_ref[rows, :], even_blk)
                if mul != 1.0:
                    z = z * mul
                for half, zh in enumerate((z[:, :HEAD_DIM], z[:, HEAD_DIM:])):
                    if per_request:
                        o_ref[sub, 2 * p + half] = zh.astype(dtype)
                    else:
                        o_ref[2 * p + half, rows, :] = zh.astype(dtype)

        heads_out(qa_o, OFF_QA, NA_HEADS // 2, 0, False, scale, BF16, False)
        heads_out(ka_o, OFF_KA, NA_HEADS // 2, 1, False, 1.0, F32, kv_per_request)
        heads_out(va_o, OFF_VA, NA_HEADS // 2, None, False, 1.0, F32, kv_per_request)
        xr_o[rows, :] = y[:, OFF_XR:OFF_XR + LRU_WIDTH]
        gr_o[rows, :] = y[:, OFF_GR:OFF_GR + LRU_WIDTH]
        heads_out(qc_o, OFF_QC, GQA_HEADS // 2, 2, rope, scale, BF16, False)
        heads_out(kc_o, OFF_KC, GQA_KV_HEADS // 2, 3, rope, 1.0, F32, kv_per_request)
        heads_out(vc_o, OFF_VC, GQA_KV_HEADS // 2, None, False, 1.0, F32, kv_per_request)


def _inproj(x, prev, mod, g, w_bf, gains, rope_tabs, t_req, kv_per_request):
    t, d = x.shape
    tm = INPROJ_TILE
    nsub = tm // TOKEN_SUBTILE
    n_req = t // t_req
    tok = pl.BlockSpec((tm, d), lambda i: (i, 0))
    modspec = _mod_spec(mod, max(t_req // tm, 1))
    in_specs = [tok, modspec, _full(g), _full(w_bf), _full(gains)]
    args = [x, mod, g, w_bf, gains]
    rope = rope_tabs is not None
    if rope:
        tpr = t_req // tm
        in_specs += [pl.BlockSpec((tm, LANES), lambda i: (i % tpr, 0))] * 2
        args += list(rope_tabs)
    has_prev = prev is not None
    if has_prev:
        ffn, mod_prev = prev
        in_specs = [tok, modspec] + in_specs
        args = [ffn, mod_prev] + args

    def hm(nh, dtype):
        return (pl.BlockSpec((nh, tm, HEAD_DIM), lambda i: (0, i, 0)),
                jax.ShapeDtypeStruct((nh, t, HEAD_DIM), dtype))

    def kv(nh):
        if not kv_per_request:
            return hm(nh, F32)
        assert t_req == TOKEN_SUBTILE
        return (pl.BlockSpec((nsub, nh, t_req, HEAD_DIM), lambda i: (i, 0, 0, 0)),
                jax.ShapeDtypeStruct((n_req, nh, t_req, HEAD_DIM), F32))

    def tk(width):
        return (pl.BlockSpec((tm, width), lambda i: (i, 0)), jax.ShapeDtypeStruct((t, width), F32))

    outs = [hm(NA_HEADS, BF16), kv(NA_HEADS), kv(NA_HEADS), tk(LRU_WIDTH), tk(LRU_WIDTH),
            hm(GQA_HEADS, BF16), kv(GQA_KV_HEADS), kv(GQA_KV_HEADS)]
    if has_prev:
        outs = [tk(d)] + outs
    res = pl.pallas_call(
        functools.partial(_inproj_kernel, has_prev, rope, kv_per_request),
        grid=(t // tm,),
        in_specs=in_specs,
        out_specs=[o[0] for o in outs],
        out_shape=[o[1] for o in outs],
        compiler_params=_cparams("arbitrary"),
        name="inproj",
    )(*args)
    if has_prev:
        return res[0], res[1:]
    return x, res


def _softmax_pv(parts):
    m = None
    for s, _ in parts:
        mi = jnp.max(s, axis=-1, keepdims=True)
        m = mi if m is None else jnp.maximum(m, mi)
    den = None
    acc = None
    for s, v in parts:
        p = jnp.exp(s - m)
        di = jnp.sum(p, axis=-1, keepdims=True)
        oi = _dot(p.astype(BF16), v)
        den = di if den is None else den + di
        acc = oi if acc is None else acc + oi
    return acc / den


def _ctx_attn_kernel(qa_ref, ka_ref, va_ref, qc_ref, kc_ref, vc_ref, oa_ref, oc_ref):
    for h in range(NA_HEADS):
        k = ka_ref[h].astype(BF16)
        v = va_ref[h].astype(BF16)
        oa_ref[h] = _softmax_pv([(_dot_nt(qa_ref[h], k), v)]).astype(BF16)
    for h in range(GQA_HEADS):
        j = h // GQA_GROUP
        k = kc_ref[j].astype(BF16)
        v = vc_ref[j].astype(BF16)
        oc_ref[h] = _softmax_pv([(_dot_nt(qc_ref[h], k), v)]).astype(BF16)


def _ctx_attention(qa, ka, va, qc, kc, vc):
    n_req, _, s, _ = ka.shape
    t = qa.shape[1]

    def hm(nh):
        return pl.BlockSpec((nh, s, HEAD_DIM), lambda b: (0, b, 0))

    def pr(nh):
        return pl.BlockSpec((None, nh, s, HEAD_DIM), lambda b: (b, 0, 0, 0))

    return pl.pallas_call(
        _ctx_attn_kernel,
        grid=(n_req,),
        in_specs=[hm(NA_HEADS), pr(NA_HEADS), pr(NA_HEADS), hm(GQA_HEADS), pr(GQA_KV_HEADS), pr(GQA_KV_HEADS)],
        out_specs=[hm(NA_HEADS), hm(GQA_HEADS)],
        out_shape=[jax.ShapeDtypeStruct((NA_HEADS, t, HEAD_DIM), BF16),
                   jax.ShapeDtypeStruct((GQA_HEADS, t, HEAD_DIM), BF16)],
        compiler_params=_cparams("arbitrary"),
        name="ctx_attention",
    )(qa, ka, va, qc, kc, vc)


def _na_lat_kernel(grid_rows, q_ref, k_ref, v_ref, ck_ref, cv_ref, *rest):
    bias_refs, (o_ref, s0, s1) = rest[:NA_BLOCKS_PER_STEP], rest[NA_BLOCKS_PER_STEP:]
    nq = NA_ROW_BLOCK * GRID_W
    nkeys = NA_KEY_ROWS * GRID_W
    sub = nq // NA_Q_SPLIT
    ck = ck_ref[...].astype(BF16)
    cv = cv_ref[...].astype(BF16)
    bufs = (s0, s1)
    chains = [(blk, part) for blk in range(NA_BLOCKS_PER_STEP) for part in range(NA_Q_SPLIT)]

    def window(blk):
        rb = pl.program_id(2) * NA_BLOCKS_PER_STEP + blk
        w0 = jnp.clip(NA_ROW_BLOCK * rb - NA_WIN_H // 2, 0, grid_rows - NA_KEY_ROWS)
        return pl.ds(pl.multiple_of(w0 * GRID_W, 4 * GRID_W), nkeys)

    def scores(n):
        blk, part = chains[n]
        rows = pl.ds(blk * nq + part * sub, sub)
        q = q_ref[rows, :]
        bias_ref = bias_refs[blk]
        buf = bufs[n % 2]
        buf[:, :nkeys] = _dot_nt(q, k_ref[window(blk), :].astype(BF16)) + bias_ref[pl.ds(part * sub, sub), :]
        buf[:, nkeys:] = _dot_nt(q, ck)

    def finish(n):
        blk, part = chains[n]
        s = bufs[n % 2][...]
        m = jnp.max(s, axis=-1, keepdims=True)
        p = jnp.exp(s - m)
        den = jnp.sum(p, axis=-1, keepdims=True)
        pb = p.astype(BF16)
        o = _dot(pb[:, :nkeys], v_ref[window(blk), :].astype(BF16)) + _dot(pb[:, nkeys:], cv)
        o_ref[pl.ds(blk * nq + part * sub, sub), :] = (o / den).astype(BF16)

    scores(0)
    for n in range(len(chains)):
        if n + 1 < len(chains):
            scores(n + 1)
        finish(n)


def _na_bias_table(rpb, grid_rows):
    nrb = grid_rows // NA_ROW_BLOCK
    nh, n_dr, n_dc = rpb.shape
    c = jnp.arange(GRID_W)[:, None]
    kc = jnp.arange(GRID_W)[None, :]
    cs = jnp.clip(c - NA_WIN_W // 2, 0, GRID_W - NA_WIN_W)
    in_win = (kc >= cs) & (kc < cs + NA_WIN_W)
    pick = (jnp.arange(n_dc)[:, None, None] == (kc - c + NA_WIN_W - 1)[None]).astype(F32)
    toep = jnp.dot(rpb.astype(F32).reshape(nh * n_dr, n_dc), pick.reshape(n_dc, GRID_W * GRID_W),
                   precision=lax.Precision.HIGHEST).reshape(nh, n_dr, GRID_W, GRID_W)
    toep = jnp.where(in_win, toep, MASK_VALUE)

    plan = []
    for rb in (0, min(1, nrb - 1), nrb - 1):
        r0 = NA_ROW_BLOCK * rb
        w0 = min(max(r0 - NA_WIN_H // 2, 0), grid_rows - NA_KEY_ROWS)
        rows = []
        for r in range(r0, r0 + NA_ROW_BLOCK):
            rs = min(max(r - NA_WIN_H // 2, 0), grid_rows - NA_WIN_H)
            rows.append([kr - r + NA_WIN_H - 1 if rs <= kr < rs + NA_WIN_H else None
                         for kr in range(w0, w0 + NA_KEY_ROWS)])
        plan.append(rows)

    def assemble(toep_ref, o_ref):
        masked = jnp.full((GRID_W, GRID_W), MASK_VALUE, F32)
        for v, rows in enumerate(plan):
            for i, row in enumerate(rows):
                for j, dr in enumerate(row):
                    o_ref[v, i * GRID_W:(i + 1) * GRID_W, j * GRID_W:(j + 1) * GRID_W] = (
                        masked if dr is None else toep_ref[dr])

    nq, nk = NA_ROW_BLOCK * GRID_W, NA_KEY_ROWS * GRID_W
    return pl.pallas_call(
        assemble,
        grid=(nh,),
        in_specs=[pl.BlockSpec((None, n_dr, GRID_W, GRID_W), lambda h: (h, 0, 0, 0))],
        out_specs=pl.BlockSpec((None, len(plan), nq, nk), lambda h: (h, 0, 0, 0)),
        out_shape=jax.ShapeDtypeStruct((nh, len(plan), nq, nk), F32),
        compiler_params=_cparams("arbitrary"),
        name="na_bias_table",
    )(toep)


def _na_lat_attention(qa, ka, va, cache_k, cache_v, layer, bias, dec_seq):
    t = qa.shape[1]
    dec_batch = t // dec_seq
    grid_rows = dec_seq // GRID_W
    nrb = grid_rows // NA_ROW_BLOCK
    nq = NA_ROW_BLOCK * GRID_W

    def variant(rb):
        return jnp.where(rb == 0, 0, jnp.where(rb == nrb - 1, 2, 1))

    nbs = NA_BLOCKS_PER_STEP
    assert nrb % nbs == 0
    nsteps = nrb // nbs
    qspec = pl.BlockSpec((None, nbs * nq, HEAD_DIM), lambda h, b, i: (h, b * nsteps + i, 0))
    kvspec = pl.BlockSpec((None, dec_seq, HEAD_DIM), lambda h, b, i: (h, b, 0))
    cspec = pl.BlockSpec((None, None, None) + cache_k.shape[3:], lambda h, b, i: (b, layer, h, 0, 0))
    bias_specs = [pl.BlockSpec((None, None) + bias.shape[2:],
                               lambda h, b, i, blk=blk: (h, variant(i * nbs + blk), 0, 0)) for blk in range(nbs)]
    nkeys = NA_KEY_ROWS * GRID_W + cache_k.shape[3]
    return pl.pallas_call(
        functools.partial(_na_lat_kernel, grid_rows),
        grid=(NA_HEADS, dec_batch, nsteps),
        in_specs=[qspec, kvspec, kvspec, cspec, cspec] + bias_specs,
        out_specs=qspec,
        out_shape=jax.ShapeDtypeStruct((NA_HEADS, t, HEAD_DIM), BF16),
        scratch_shapes=[pltpu.VMEM((nq // NA_Q_SPLIT, nkeys), F32)] * 2,
        compiler_params=_cparams("arbitrary", "arbitrary", "arbitrary"),
        name="na_lat_attention",
    )(qa, ka, va, cache_k, cache_v, *([bias] * nbs))


def _gqa_lat_kernel(q_ref, k_ref, v_ref, ck_ref, cv_ref, o_ref, *scratch):
    _, tq, hd = q_ref.shape
    g = GQA_GROUP
    nh = GQA_KV_HEADS
    ch = GQA_KEY_CHUNK
    nc = k_ref.shape[1] // ch
    s_bufs = [scratch[2 * j:2 * j + 2] for j in range(nh)]
    qs = [q_ref[j * g:(j + 1) * g].reshape(g * tq, hd) for j in range(nh)]

    def scores(j, c):
        return _dot_nt(qs[j], k_ref[j, pl.ds(c * ch, ch), :].astype(BF16))

    def update(j, state, s, v):
        m, l, acc = state
        m_new = jnp.maximum(m, jnp.max(s, axis=-1, keepdims=True))
        alpha = jnp.exp(m - m_new)
        p = jnp.exp(s - m_new)
        l = alpha * l + jnp.sum(p, axis=-1, keepdims=True)
        acc = alpha * acc + _dot(p.astype(BF16), v)
        return m_new, l, acc

    states = []
    for j in range(nh):
        s = _dot_nt(qs[j], ck_ref[j].astype(BF16))
        m = jnp.max(s, axis=-1, keepdims=True)
        p = jnp.exp(s - m)
        states.append((m, jnp.sum(p, axis=-1, keepdims=True), _dot(p.astype(BF16), cv_ref[j].astype(BF16))))
        s_bufs[j][0][...] = scores(j, 0)

    for c in range(nc):
        for j in range(nh):
            if c + 1 < nc:
                s_bufs[j][(c + 1) % 2][...] = scores(j, c + 1)
            v = v_ref[j, pl.ds(c * ch, ch), :].astype(BF16)
            states[j] = update(j, states[j], s_bufs[j][c % 2][...], v)
    for j in range(nh):
        m, l, acc = states[j]
        o_ref[j * g:(j + 1) * g] = (acc / l).reshape(g, tq, hd).astype(BF16)


def _gqa_lat_attention(qc, kc, vc, cache_k, cache_v, layer, dec_seq):
    t = qc.shape[1]
    dec_batch = t // dec_seq
    tq = GQA_Q_TILE
    nq = dec_seq // tq
    qspec = pl.BlockSpec((GQA_HEADS, tq, HEAD_DIM), lambda b, i: (0, b * nq + i, 0))
    kvspec = pl.BlockSpec((GQA_KV_HEADS, dec_seq, HEAD_DIM), lambda b, i: (0, b, 0))
    cspec = pl.BlockSpec((None, None) + cache_k.shape[2:], lambda b, i: (b, layer, 0, 0, 0))
    return pl.pallas_call(
        _gqa_lat_kernel,
        grid=(dec_batch, nq),
        in_specs=[qspec, kvspec, kvspec, cspec, cspec],
        out_specs=qspec,
        out_shape=jax.ShapeDtypeStruct((GQA_HEADS, t, HEAD_DIM), BF16),
        scratch_shapes=[pltpu.VMEM((GQA_GROUP * tq, GQA_KEY_CHUNK), F32)] * (2 * GQA_KV_HEADS),
        compiler_params=_cparams("arbitrary", "arbitrary"),
        name="gqa_lat_attention",
    )(qc, kc, vc, cache_k, cache_v)


LRU_CHUNK = 256
LRU_PAD = SUBLANES


def _lru_kernel(xr_ref, gr_ref, h0_ref, cw_ref, cb_ref, w_ref, b_ref, lam_ref,
                y_ref, st_ref, xpad, a_f, u_f, a_b, u_b):
    t, w = xr_ref.shape
    nch = t // LRU_CHUNK
    ngrp = t // SUBLANES

    zeros = jnp.zeros((LRU_PAD, w), F32)
    xpad[pl.ds(0, LRU_PAD), :] = zeros
    xpad[pl.ds(LRU_PAD + t, LRU_PAD), :] = zeros
    xpad[pl.ds(LRU_PAD, t), :] = xr_ref[...]

    lam = lam_ref[...]
    decay = -LRU_C * (jnp.maximum(-lam, 0.0) + jnp.log1p(jnp.exp(-jnp.abs(lam))))
    h0 = h0_ref[...]
    row = lax.broadcasted_iota(I32, (LRU_CHUNK, 1), 0)
    srow = lax.broadcasted_iota(I32, (SUBLANES, 1), 0)
    left = CONV_W // 2

    def gates(c, _):
        t0 = pl.multiple_of(c * LRU_CHUNK, LRU_CHUNK)
        cur = xpad[pl.ds(t0 + LRU_PAD, LRU_CHUNK), :]
        before = xpad[pl.ds(t0, LRU_PAD), :]
        after = xpad[pl.ds(t0 + LRU_PAD + LRU_CHUNK, LRU_PAD), :]
        xc = cb_ref[...]
        for j in range(CONV_W):
            s = left - j
            if s > 0:
                sh = pltpu.roll(cur, s, 0)
                head = jnp.where(srow < s, pltpu.roll(before, s, 0), sh[:SUBLANES])
                tap = jnp.concatenate([head, sh[SUBLANES:]], axis=0)
            elif s < 0:
                sh = pltpu.roll(cur, LRU_CHUNK + s, 0)
                tail = jnp.where(srow >= SUBLANES + s, pltpu.roll(after, SUBLANES + s, 0), sh[-SUBLANES:])
                tap = jnp.concatenate([sh[:-SUBLANES], tail], axis=0)
            else:
                tap = cur
            xc = xc + tap * cw_ref[j:j + 1, :]
        z = _dot(xc.astype(BF16), w_ref[...]) + b_ref[...]
        for d, (a_ref, u_ref, edge_chunk, edge_row) in enumerate(
                ((a_f, u_f, 0, 0), (a_b, u_b, nch - 1, LRU_CHUNK - 1))):
            r = jax.nn.sigmoid(z[:, (2 * d) * w:(2 * d + 1) * w])
            i = jax.nn.sigmoid(z[:, (2 * d + 1) * w:(2 * d + 2) * w])
            log_a = decay[d:d + 1, :] * r
            a = jnp.exp(log_a)
            u = jnp.sqrt(-jnp.tanh(log_a) * (a * a + 1.0)) * (i * xc)
            first = (row == edge_row) & (c == edge_chunk)
            u = u + jnp.where(first, a * h0[d:d + 1, :], 0.0)
            a_ref[pl.ds(t0, LRU_CHUNK), :] = a
            u_ref[pl.ds(t0, LRU_CHUNK), :] = u
        return 0

    lax.fori_loop(0, nch, gates, 0)

    def scan(i, carry):
        hf, hb = carry
        o = pl.multiple_of(i * SUBLANES, SUBLANES)
        a = a_f[pl.ds(o, SUBLANES), :]
        u = u_f[pl.ds(o, SUBLANES), :]
        for s in (1, 2, 4):
            keep = srow >= s
            a_s = jnp.where(keep, pltpu.roll(a, s, 0), 1.0)
            u_s = jnp.where(keep, pltpu.roll(u, s, 0), 0.0)
            u = a * u_s + u
            a = a * a_s
        h = u + a * hf
        u_f[pl.ds(o, SUBLANES), :] = h
        hf = h[SUBLANES - 1:SUBLANES, :]
        o = pl.multiple_of((ngrp - 1 - i) * SUBLANES, SUBLANES)
        a = a_b[pl.ds(o, SUBLANES), :]
        u = u_b[pl.ds(o, SUBLANES), :]
        for s in (1, 2, 4):
            keep = srow < SUBLANES - s
            a_s = jnp.where(keep, pltpu.roll(a, SUBLANES - s, 0), 1.0)
            u_s = jnp.where(keep, pltpu.roll(u, SUBLANES - s, 0), 0.0)
            u = a * u_s + u
            a = a * a_s
        h = u + a * hb
        u_b[pl.ds(o, SUBLANES), :] = h
        hb = h[0:1, :]
        return hf, hb

    zero_row = jnp.zeros((1, w), F32)
    hf, hb = lax.fori_loop(0, ngrp, scan, (zero_row, zero_row))
    st_ref[0:1, :] = hf
    st_ref[1:2, :] = hb

    def emit(c, _):
        t0 = pl.multiple_of(c * LRU_CHUNK, LRU_CHUNK)
        hs = u_f[pl.ds(t0, LRU_CHUNK), :] + u_b[pl.ds(t0, LRU_CHUNK), :]
        y = jax.nn.gelu(gr_ref[pl.ds(t0, LRU_CHUNK), :]) * hs
        y_ref[pl.ds(t0, LRU_CHUNK), :] = y.astype(y_ref.dtype)
        return 0

    lax.fori_loop(0, nch, emit, 0)


def _lru(xr, gr, h0, cw, cb, w_bf, b_cat, lam, t_req):
    t, w = xr.shape
    n_req = t // t_req
    tok = pl.BlockSpec((t_req, w), lambda b: (b, 0))
    st = pl.BlockSpec((None, 2, w), lambda b: (b, 0, 0))
    return pl.pallas_call(
        _lru_kernel,
        grid=(n_req,),
        in_specs=[tok, tok, st, _full(cw), _full(cb), _full(w_bf), _full(b_cat), _full(lam)],
        out_specs=[tok, st],
        out_shape=[jax.ShapeDtypeStruct((t, w), BF16), jax.ShapeDtypeStruct((n_req, 2, w), F32)],
        scratch_shapes=[pltpu.VMEM((t_req + 2 * LRU_PAD, w), F32)] + [pltpu.VMEM((t_req, w), F32)] * 4,
        compiler_params=_cparams("arbitrary"),
        name="rglru",
    )(xr, gr, h0, cw, cb, w_bf, b_cat, lam)


def _outproj_kernel(x_ref, oa_ref, ob_ref, oc_ref, mod_ref, w_ref, gf_ref, rw_ref,
                    xm_ref, u_ref, aff_ref):
    d = x_ref.shape[-1]
    g1 = mod_ref[:, 2 * d:3 * d]
    sh2 = mod_ref[:, 3 * d:4 * d]
    sc2 = mod_ref[:, 4 * d:5 * d]
    rw = rw_ref[...]
    r_hi = rw.astype(BF16)
    r_lo = (rw - r_hi.astype(F32)).astype(BF16)
    for sub in range(x_ref.shape[0] // TOKEN_SUBTILE):
        rows = pl.ds(sub * TOKEN_SUBTILE, TOKEN_SUBTILE)
        pieces = ([oa_ref[h, rows, :].astype(F32) for h in range(NA_HEADS)] + [ob_ref[rows, :].astype(F32)]
                  + [oc_ref[h, rows, :].astype(F32) for h in range(GQA_HEADS)])
        o = jnp.concatenate(pieces, axis=-1).astype(BF16)
        mix = _dot(o, w_ref[...])
        xm = x_ref[rows, :] + g1 * mix
        xm_ref[rows, :] = xm
        u = _rms(xm, gf_ref[...]) * (1.0 + sc2) + sh2
        u_ref[rows, :] = u
        u_hi = u.astype(BF16)
        u_lo = (u - u_hi.astype(F32)).astype(BF16)
        lg = _dot_nt(r_hi, u_hi) + (_dot_nt(r_lo, u_hi) + _dot_nt(r_hi, u_lo))
        m = jnp.max(lg, axis=0, keepdims=True)
        e = jnp.exp(lg - m)
        aff_ref[:, rows] = e / jnp.sum(e, axis=0, keepdims=True)


def _outproj(x, oa, ob, oc, mod, w_bf, gf, rw_t, t_req):
    t, d = x.shape
    tm = TOKEN_TILE
    tpr = max(t_req // tm, 1)
    tok = pl.BlockSpec((tm, d), lambda i: (i, 0))
    hm = pl.BlockSpec((NA_HEADS, tm, HEAD_DIM), lambda i: (0, i, 0))
    return pl.pallas_call(
        _outproj_kernel,
        grid=(t // tm,),
        in_specs=[tok, hm, pl.BlockSpec((tm, LRU_WIDTH), lambda i: (i, 0)), hm,
                  _mod_spec(mod, tpr), _full(w_bf), _full(gf), _full(rw_t)],
        out_specs=[tok, tok, pl.BlockSpec((N_EXPERTS, tm), lambda i: (0, i))],
        out_shape=[jax.ShapeDtypeStruct((t, d), F32), jax.ShapeDtypeStruct((t, d), F32),
                   jax.ShapeDtypeStruct((N_EXPERTS, t), F32)],
        compiler_params=_cparams("arbitrary"),
        name="outproj_router",
    )(x, oa, ob, oc, mod, w_bf, gf, rw_t)


PREFIX_BLOCK = 256
TOPK_REQUESTS_PER_STEP = 8


def _topk_kernel(cap, t_req, compact, aff_ref, *outs):
    ne, tt = aff_ref.shape
    ng = tt // t_req
    slot_ref = outs[0]
    keys = [pltpu.bitcast(aff_ref[:, g * t_req:(g + 1) * t_req], I32) for g in range(ng)]

    def search(it, thrs):
        bit = jnp.left_shift(jnp.int32(1), 30 - it)
        out = []
        for g in range(ng):
            cand = thrs[g] | bit
            cnt = jnp.sum((keys[g] >= cand).astype(F32), axis=-1, keepdims=True)
            out.append(jnp.where(cnt >= cap, cand, thrs[g]))
        return tuple(out)

    thrs = lax.fori_loop(0, 31, search, tuple(jnp.zeros((ne, 1), I32) for _ in range(ng)))

    ri = lax.broadcasted_iota(I32, (PREFIX_BLOCK, PREFIX_BLOCK), 0)
    ci = lax.broadcasted_iota(I32, (PREFIX_BLOCK, PREFIX_BLOCK), 1)
    tri = jnp.where(ri <= ci, 1.0, 0.0).astype(BF16)

    def prefix(mask_f32):
        carry = jnp.zeros((ne, 1), F32)
        blocks = []
        for b in range(t_req // PREFIX_BLOCK):
            blk = mask_f32[:, b * PREFIX_BLOCK:(b + 1) * PREFIX_BLOCK]
            blocks.append(_dot(blk.astype(BF16), tri) + carry)
            carry = carry + jnp.sum(blk, axis=-1, keepdims=True)
        return blocks[0] if len(blocks) == 1 else jnp.concatenate(blocks, axis=-1)

    for g in range(ng):
        gt = keys[g] > thrs[g]
        eq = keys[g] == thrs[g]
        need = cap - jnp.sum(gt.astype(F32), axis=-1, keepdims=True)
        tie_rank = prefix(eq.astype(F32))
        sel = jnp.where(gt, 1.0, jnp.where(eq, (tie_rank <= need).astype(F32), 0.0))
        slot = jnp.where(sel > 0.0, prefix(sel), 0.0)
        slot_ref[:, g * t_req:(g + 1) * t_req] = slot

    if not compact:
        return
    assert ng == 1
    idx_ref, gate_ref = outs[1], outs[2]
    tok = lax.broadcasted_iota(I32, (ne, t_req), 1)
    valid = (slot > 0.0).astype(I32)
    dist = jnp.where(slot > 0.0, tok - (slot.astype(I32) - 1), 0)
    tokv = tok
    gate = aff_ref[...]
    for k in range((t_req - 1).bit_length()):
        sh = 1 << k
        nb = lambda x: pltpu.roll(x, t_req - sh, 1)
        n_valid, n_dist = nb(valid), nb(dist)
        incoming = (n_valid * ((n_dist >> k) & 1)) > 0
        stay = valid * (1 - ((dist >> k) & 1))
        tokv = jnp.where(incoming, nb(tokv), tokv)
        gate = jnp.where(incoming, nb(gate), gate)
        dist = jnp.where(incoming, n_dist, dist)
        valid = jnp.where(incoming, 1, stay)
    idx_ref[...] = tokv[:, :cap]
    gate_ref[...] = gate[:, :cap]


def _topk(aff, t_req, cap, compact):
    ne, t = aff.shape
    n_req = t // t_req
    ng = 1 if compact else min(TOPK_REQUESTS_PER_STEP, n_req)
    assert n_req % ng == 0
    lanes = ng * t_req
    out_specs = [pl.BlockSpec((ne, lanes), lambda b: (0, b))]
    out_shape = [jax.ShapeDtypeStruct((ne, t), F32)]
    if compact:
        out_specs += [pl.BlockSpec((None, ne, cap), lambda b: (b, 0, 0))] * 2
        out_shape += [jax.ShapeDtypeStruct((n_req, ne, cap), I32), jax.ShapeDtypeStruct((n_req, ne, cap), F32)]
    return pl.pallas_call(
        functools.partial(_topk_kernel, cap, t_req, compact),
        grid=(n_req // ng,),
        in_specs=[pl.BlockSpec((ne, lanes), lambda b: (0, b))],
        out_specs=out_specs,
        out_shape=out_shape,
        compiler_params=_cparams("arbitrary"),
        name="expert_topk",
    )(aff)


def _slot_onehot(slot_ref, cap):
    ne = slot_ref.shape[0]
    want = (lax.broadcasted_iota(I32, (cap, 1), 0) + 1).astype(F32)
    return jnp.concatenate([jnp.where(slot_ref[e:e + 1, :] == want, 1.0, 0.0) for e in range(ne)], axis=0)


def _ctx_dispatch_kernel(slot_ref, u_ref, x_ref):
    ne, cap, d = x_ref.shape
    onehot = _slot_onehot(slot_ref, cap).astype(BF16)
    x = _dot(onehot, u_ref[...].astype(BF16))
    x_ref[...] = x.reshape(ne, cap, d).astype(BF16)


def _ctx_dispatch(slot, u, t_req, cap):
    ne, t = slot.shape
    n_req = t // t_req
    d = u.shape[-1]
    return pl.pallas_call(
        _ctx_dispatch_kernel,
        grid=(n_req,),
        in_specs=[pl.BlockSpec((ne, t_req), lambda b: (0, b)),
                  pl.BlockSpec((t_req, d), lambda b: (b, 0))],
        out_specs=pl.BlockSpec((ne, None, cap, d), lambda b: (0, b, 0, 0)),
        out_shape=jax.ShapeDtypeStruct((ne, n_req, cap, d), BF16),
        compiler_params=_cparams("arbitrary"),
        name="ctx_dispatch",
    )(slot, u)


def _ctx_combine_kernel(slot_ref, aff_ref, y_ref, o_ref):
    ne, cap, d = y_ref.shape
    want = (lax.broadcasted_iota(I32, (cap, 1), 0) + 1).astype(F32)
    hots, gates = [], []
    for e in range(ne):
        hot = jnp.where(slot_ref[e:e + 1, :] == want, 1.0, 0.0)
        hots.append(hot)
        gates.append(jnp.sum(hot * aff_ref[e:e + 1, :], axis=-1, keepdims=True))
    onehot = jnp.concatenate(hots, axis=0).astype(BF16)
    y = y_ref[...].reshape(ne * cap, d) * jnp.concatenate(gates, axis=0)
    y_hi = y.astype(BF16)
    y_lo = (y - y_hi.astype(F32)).astype(BF16)
    tn = (((0,), (0,)), ((), ()))
    o_ref[...] = (lax.dot_general(onehot, y_hi, tn, preferred_element_type=F32)
                  + lax.dot_general(onehot, y_lo, tn, preferred_element_type=F32))


def _ctx_combine(slot, aff, ye, t_req):
    ne, n_req, cap, d = ye.shape
    lane = pl.BlockSpec((ne, t_req), lambda b: (0, b))
    return pl.pallas_call(
        _ctx_combine_kernel,
        grid=(n_req,),
        in_specs=[lane, lane, pl.BlockSpec((ne, None, cap, d), lambda b: (0, b, 0, 0))],
        out_specs=pl.BlockSpec((t_req, d), lambda b: (b, 0)),
        out_shape=jax.ShapeDtypeStruct((n_req * t_req, d), F32),
        compiler_params=_cparams("arbitrary"),
        name="ctx_combine",
    )(slot, aff, ye)


SC_GATHER_ROWS = 64


def _lat_dispatch(gidx, u):
    ne, rows = gidx.shape
    d = u.shape[-1]
    total = ne * rows
    mesh = plsc.VectorSubcoreMesh(core_axis_name="c", subcore_axis_name="s")
    n_cores = mesh.num_cores
    n_workers = n_cores * mesh.num_subcores
    chunk = SC_GATHER_ROWS
    per_worker = total // n_workers
    assert total % n_workers == 0 and per_worker % chunk == 0

    @functools.partial(
        pl.kernel, mesh=mesh, out_type=jax.ShapeDtypeStruct((total, d), u.dtype),
        scratch_types=[pltpu.VMEM((chunk,), I32), pltpu.VMEM((chunk, d), u.dtype), pltpu.SemaphoreType.DMA])
    def gather(u_hbm, idx_hbm, out_hbm, idx_v, rows_v, sem):
        worker = lax.axis_index("s") * n_cores + lax.axis_index("c")
        base = worker * per_worker

        @pl.loop(0, per_worker // chunk)
        def _(i):
            off = pl.multiple_of(base + i * chunk, SUBLANES)
            pltpu.sync_copy(idx_hbm.at[pl.ds(off, chunk)], idx_v)
            pltpu.async_copy(u_hbm.at[idx_v], rows_v, sem).wait()
            pltpu.sync_copy(rows_v, out_hbm.at[pl.ds(off, chunk)])

    return gather(u, gidx.reshape(total)).reshape(ne, rows, d)


def _ffn_kernel(nt_ctx, xc_ref, xl_ref, wg_ref, wu_ref, wd_ref, yc_ref, yl_ref, wg_bf, wu_bf, wd_bf):
    j = pl.program_id(1)

    @pl.when(j == 0)
    def _():
        wg_bf[...] = wg_ref[...].astype(BF16)
        wu_bf[...] = wu_ref[...].astype(BF16)
        wd_bf[...] = wd_ref[...].astype(BF16)

    def ffn(x):
        h = jax.nn.silu(_dot(x, wg_bf[...])) * _dot(x, wu_bf[...])
        return _dot(h.astype(BF16), wd_bf[...])

    @pl.when(j < nt_ctx)
    def _():
        yc_ref[...] = ffn(xc_ref[...])

    @pl.when(j >= nt_ctx)
    def _():
        yl_ref[...] = ffn(xl_ref[...].astype(BF16))


def _expert_ffn(x_ctx, x_lat, wg, wu, wd, layer):
    ne, rows_ctx, d = x_ctx.shape
    rows_lat = x_lat.shape[1]
    ff = wg.shape[-1]
    tr = FFN_ROW_TILE
    nt_ctx, nt_lat = rows_ctx // tr, rows_lat // tr
    wspec = lambda a: pl.BlockSpec((None, None) + a.shape[2:], lambda e, j: (layer, e, 0, 0))
    ctx_spec = pl.BlockSpec((None, tr, d), lambda e, j: (e, jnp.minimum(j, nt_ctx - 1), 0))
    lat_spec = pl.BlockSpec((None, tr, d), lambda e, j: (e, jnp.maximum(j - nt_ctx, 0), 0))
    return pl.pallas_call(
        functools.partial(_ffn_kernel, nt_ctx),
        grid=(ne, nt_ctx + nt_lat),
        in_specs=[ctx_spec, lat_spec, wspec(wg), wspec(wu), wspec(wd)],
        out_specs=[ctx_spec, lat_spec],
        out_shape=[jax.ShapeDtypeStruct((ne, rows_ctx, d), F32), jax.ShapeDtypeStruct((ne, rows_lat, d), F32)],
        scratch_shapes=[pltpu.VMEM((d, ff), BF16), pltpu.VMEM((d, ff), BF16), pltpu.VMEM((ff, d), BF16)],
        compiler_params=_cparams("arbitrary", "arbitrary"),
        name="expert_ffn",
    )(x_ctx, x_lat, wg, wu, wd)


COMBINE_GROUP = 4


def _lat_combine_kernel(idx_ref, gate_ref, y_ref, o_ref):
    e = pl.program_id(1)
    cap = y_ref.shape[0]

    @pl.when(e == 0)
    def _():
        o_ref[...] = jnp.zeros(o_ref.shape, o_ref.dtype)

    def body(q, _):
        r0 = q * COMBINE_GROUP
        toks = [idx_ref[0, r0 + k] for k in range(COMBINE_GROUP)]
        acc = [o_ref[pl.ds(i, 1), :] for i in toks]
        for k in range(COMBINE_GROUP):
            o_ref[pl.ds(toks[k], 1), :] = acc[k] + gate_ref[0, r0 + k] * y_ref[pl.ds(r0 + k, 1), :]
        return 0

    lax.fori_loop(0, cap // COMBINE_GROUP, body, 0)


def _lat_combine(idx, gate, ye, t_req):
    n_req, ne, _, cap = idx.shape
    d = ye.shape[-1]
    sspec = pl.BlockSpec((None, None, 1, cap), lambda b, e: (b, e, 0, 0), memory_space=pltpu.SMEM)
    return pl.pallas_call(
        _lat_combine_kernel,
        grid=(n_req, ne),
        in_specs=[sspec, sspec, pl.BlockSpec((None, cap, d), lambda b, e: (e, b, 0))],
        out_specs=pl.BlockSpec((t_req, d), lambda b, e: (b, 0)),
        out_shape=jax.ShapeDtypeStruct((n_req * t_req, d), F32),
        compiler_params=_cparams("arbitrary", "arbitrary"),
        name="lat_combine",
    )(idx, gate, ye)


def _final_kernel(xm_ref, f_ref, mod_ref, o_ref):
    d = xm_ref.shape[-1]
    o_ref[...] = xm_ref[...] + mod_ref[:, 5 * d:6 * d] * f_ref[...]


def _final(xm, ffn, mod, t_req):
    t, d = xm.shape
    tm = TOKEN_TILE
    tpr = max(t_req // tm, 1)
    tok = pl.BlockSpec((tm, d), lambda i: (i, 0))
    return pl.pallas_call(
        _final_kernel,
        grid=(t // tm,),
        in_specs=[tok, tok, _mod_spec(mod, tpr)],
        out_specs=tok,
        out_shape=jax.ShapeDtypeStruct((t, d), F32),
        compiler_params=_cparams("arbitrary"),
        name="final_residual",
    )(xm, ffn, mod)


def _rope_tables(dec_seq):
    pos = jnp.arange(dec_seq)
    n = HEAD_DIM // 4
    inv = ROPE_THETA ** (-jnp.arange(n, dtype=F32) / n)
    ang_r = (pos // GRID_W).astype(F32)[:, None] * inv[None, :]
    ang_c = (pos % GRID_W).astype(F32)[:, None] * inv[None, :]
    cr, sr, cc, sc = jnp.cos(ang_r), jnp.sin(ang_r), jnp.cos(ang_c), jnp.sin(ang_c)
    cos = jnp.concatenate([cr, cr, cc, cc] * 2, axis=-1)
    sin = jnp.concatenate([-sr, sr, -sc, sc] * 2, axis=-1)
    return cos, sin


def _block_diag(wts):
    nb, bw, _ = wts.shape
    eye = jnp.eye(nb, dtype=wts.dtype)
    return (wts[:, :, None, :] * eye[:, None, :, None]).reshape(nb * bw, nb * bw)


def kernel(x_prompt, x_sample, c, cache_na_k, cache_na_v, cache_gqa_k, cache_gqa_v, state_lru, c_ctx, mod_w, mod_b, norm_mix_g, norm_ffn_g, w_in, na_q_norm_g, na_k_norm_g, na_rpb, conv_w, conv_b, lru_w_a, lru_b_a, lru_w_i, lru_b_i, lru_lambda, gqa_q_norm_g, gqa_k_norm_g, w_out, router_w, expert_w_gate, expert_w_up, expert_w_down):
    batch, seq, d = x_prompt.shape
    dec_batch, dec_seq, _ = x_sample.shape
    depth = mod_w.shape[0]
    t_ctx = batch * seq
    t_lat = dec_batch * dec_seq
    assert seq == TOKEN_SUBTILE and dec_seq % TOKEN_TILE == 0 and (batch * seq) % TOKEN_TILE == 0
    assert (dec_seq // GRID_W) % NA_ROW_BLOCK == 0 and dec_seq // GRID_W >= NA_KEY_ROWS
    cap_ctx = EC_FACTOR * seq // N_EXPERTS
    cap_lat = EC_FACTOR * dec_seq // N_EXPERTS
    rows_ctx = batch * cap_ctx
    rows_lat = dec_batch * cap_lat
    assert rows_ctx % FFN_ROW_TILE == 0 and rows_lat % FFN_ROW_TILE == 0 and cap_lat % COMBINE_GROUP == 0

    n_mod_rows = SUBLANES
    assert 1 + dec_batch <= n_mod_rows
    cvec = jnp.zeros((n_mod_rows, d), F32).at[0].set(c_ctx).at[1:1 + dec_batch].set(c)
    mods = _modulation(cvec, mod_w, mod_b)

    rope_tabs = _rope_tables(dec_seq)
    x_c = x_prompt.reshape(t_ctx, d)
    x_l = x_sample.reshape(t_lat, d)
    zeros_h0 = jnp.zeros((batch, 2, LRU_WIDTH), F32)

    new_na_k, new_na_v, new_gqa_k, new_gqa_v, new_lru = [], [], [], [], []
    prev_c = prev_l = None
    for l in range(depth):
        mod_c = mods[l, 0:1].reshape(1, 1, 6 * d)
        mod_l = mods[l, 1:1 + dec_batch].reshape(dec_batch, 1, 6 * d)
        tile2 = lambda g: jnp.concatenate([g, g])
        gains = jnp.stack([tile2(na_q_norm_g[l]), tile2(na_k_norm_g[l]),
                           tile2(gqa_q_norm_g[l]), tile2(gqa_k_norm_g[l])])
        w_in_bf = w_in[l].astype(BF16)
        g_mix = norm_mix_g[l][None, :]
        x_c, (qa_c, ka_c, va_c, xr_c, gr_c, qc_c, kc_c, vc_c) = _inproj(
            x_c, prev_c, mod_c, g_mix, w_in_bf, gains, None, seq, True)
        x_l, (qa_l, ka_l, va_l, xr_l, gr_l, qc_l, kc_l, vc_l) = _inproj(
            x_l, prev_l, mod_l, g_mix, w_in_bf, gains, rope_tabs, dec_seq, False)

        oa_c, oc_c = _ctx_attention(qa_c, ka_c, va_c, qc_c, kc_c, vc_c)
        bias = _na_bias_table(na_rpb[l], dec_seq // GRID_W)
        oa_l = _na_lat_attention(qa_l, ka_l, va_l, cache_na_k, cache_na_v, l, bias, dec_seq)
        oc_l = _gqa_lat_attention(qc_l, kc_l, vc_l, cache_gqa_k, cache_gqa_v, l, dec_seq)

        w_gate = jnp.concatenate([_block_diag(lru_w_a[l, 0]), _block_diag(lru_w_i[l, 0]),
                                  _block_diag(lru_w_a[l, 1]), _block_diag(lru_w_i[l, 1])], axis=1).astype(BF16)
        b_gate = jnp.concatenate([lru_b_a[l, 0], lru_b_i[l, 0], lru_b_a[l, 1], lru_b_i[l, 1]])[None, :]
        lru_args = (conv_w[l], conv_b[l][None, :], w_gate, b_gate, lru_lambda[l])
        ob_c, st = _lru(xr_c, gr_c, zeros_h0, *lru_args, seq)
        ob_l, _ = _lru(xr_l, gr_l, state_lru[:, l], *lru_args, dec_seq)

        w_out_bf = w_out[l].astype(BF16)
        g_ffn = norm_ffn_g[l][None, :]
        rw_t = router_w[l].T
        xm_c, u_c, aff_c = _outproj(x_c, oa_c, ob_c, oc_c, mod_c, w_out_bf, g_ffn, rw_t, seq)
        xm_l, u_l, aff_l = _outproj(x_l, oa_l, ob_l, oc_l, mod_l, w_out_bf, g_ffn, rw_t, dec_seq)

        (slot_c,) = _topk(aff_c, seq, cap_ctx, False)
        _, idx_l, gate_l = _topk(aff_l, dec_seq, cap_lat, True)
        g_l = idx_l + (jnp.arange(dec_batch, dtype=I32) * dec_seq)[:, None, None]
        gidx_l = g_l.transpose(1, 0, 2).reshape(N_EXPERTS, rows_lat)
        idx_l = idx_l.reshape(dec_batch, N_EXPERTS, 1, cap_lat)
        gate_l = gate_l.reshape(dec_batch, N_EXPERTS, 1, cap_lat)

        xe_c = _ctx_dispatch(slot_c, u_c, seq, cap_ctx).reshape(N_EXPERTS, rows_ctx, d)
        xe_l = _lat_dispatch(gidx_l, u_l)
        ye_c, ye_l = _expert_ffn(xe_c, xe_l, expert_w_gate, expert_w_up, expert_w_down, l)
        ffn_c = _ctx_combine(slot_c, aff_c, ye_c.reshape(N_EXPERTS, batch, cap_ctx, d), seq)
        ffn_l = _lat_combine(idx_l, gate_l, ye_l, dec_seq)
        prev_c, prev_l = (ffn_c, mod_c), (ffn_l, mod_l)
        x_c, x_l = xm_c, xm_l

        new_na_k.append(ka_c)
        new_na_v.append(va_c)
        new_gqa_k.append(kc_c)
        new_gqa_v.append(vc_c)
        new_lru.append(st)

    y_p = _final(x_c, prev_c[0], prev_c[1], seq)
    y_s = _final(x_l, prev_l[0], prev_l[1], dec_seq)
    return (y_p.reshape(batch, seq, d), y_s.reshape(dec_batch, dec_seq, d),
            jnp.stack(new_na_k, axis=1), jnp.stack(new_na_v, axis=1),
            jnp.stack(new_gqa_k, axis=1), jnp.stack(new_gqa_v, axis=1),
            jnp.stack(new_lru, axis=1))
```

```python
import functools

import jax
import jax.numpy as jnp
from jax import lax
from jax.experimental import pallas as pl
from jax.experimental.pallas import tpu as pltpu
from jax.experimental.pallas import tpu_sc as plsc

F32 = jnp.float32
BF16 = jnp.bfloat16
I32 = jnp.int32

HEAD_DIM = 64
NA_HEADS = 6
GQA_HEADS = 6
GQA_KV_HEADS = 2
GQA_GROUP = GQA_HEADS // GQA_KV_HEADS
LRU_WIDTH = 256
LRU_C = 8.0
CONV_W = 4
GRID_W = 64
NA_WIN_H = 8
NA_WIN_W = 16
N_EXPERTS = 16
EC_FACTOR = 2
ROPE_THETA = 10000.0
EPS = 1e-6
LOG2_E = 1.4426950408889634
NA_DIM = NA_HEADS * HEAD_DIM
GQA_Q_DIM = GQA_HEADS * HEAD_DIM
GQA_KV_DIM = GQA_KV_HEADS * HEAD_DIM
OFF_QA = 0
OFF_KA = OFF_QA + NA_DIM
OFF_VA = OFF_KA + NA_DIM
OFF_XR = OFF_VA + NA_DIM
OFF_GR = OFF_XR + LRU_WIDTH
OFF_QC = OFF_GR + LRU_WIDTH
OFF_KC = OFF_QC + GQA_Q_DIM
OFF_VC = OFF_KC + GQA_KV_DIM

LANES = 128
SUBLANES = 8
VMEM_LIMIT_BYTES = 56 * 1024 * 1024

TOKEN_TILE = 512
TOKEN_SUBTILE = 256
INPROJ_TILE = 256
NA_ROW_BLOCK = 4
NA_KEY_ROWS = NA_ROW_BLOCK + NA_WIN_H
NA_BLOCKS_PER_STEP = 4
NA_Q_SPLIT = 1
GQA_Q_TILE = 256
GQA_KEY_CHUNK = 256
GQA_DEN_COLS = 16
FFN_ROW_TILE = 512
MASK_VALUE = -1e30


def _cparams(*sem, flags=None):
    return pltpu.CompilerParams(dimension_semantics=sem, vmem_limit_bytes=VMEM_LIMIT_BYTES, flags=flags)


def _dot(a, b):
    return jnp.dot(a, b, preferred_element_type=F32)


def _dot_nt(a, b):
    return lax.dot_general(a, b, (((1,), (1,)), ((), ())), preferred_element_type=F32)


def _rms(x, g):
    ms = jnp.mean(x * x, axis=-1, keepdims=True)
    return x * lax.rsqrt(ms + EPS) * g


def _full(a):
    nd = a.ndim
    return pl.BlockSpec(a.shape, lambda *_: (0,) * nd)


def _mod_spec(mod, tiles_per_request):
    blk = (None, 1, mod.shape[-1])
    if mod.shape[0] == 1:
        return pl.BlockSpec(blk, lambda i: (0, 0, 0))
    return pl.BlockSpec(blk, lambda i: (i // tiles_per_request, 0, 0))


def _mod_kernel(c_ref, w_ref, b_ref, o_ref):
    s = jax.nn.silu(c_ref[...])
    o_ref[...] = jnp.dot(s, w_ref[...], precision=lax.Precision.HIGHEST,
                         preferred_element_type=F32) + b_ref[...]


def _modulation(cvec, mod_w, mod_b):
    depth, d, n = mod_w.shape
    rows = cvec.shape[0]
    tn = 1536
    return pl.pallas_call(
        _mod_kernel,
        grid=(depth, n // tn),
        in_specs=[
            pl.BlockSpec((rows, d), lambda l, j: (0, 0)),
            pl.BlockSpec((None, d, tn), lambda l, j: (l, 0, j)),
            pl.BlockSpec((None, 1, tn), lambda l, j: (l, 0, j)),
        ],
        out_specs=pl.BlockSpec((None, rows, tn), lambda l, j: (l, 0, j)),
        out_shape=jax.ShapeDtypeStruct((depth, rows, n), F32),
        compiler_params=_cparams("arbitrary", "arbitrary"),
        name="modulation",
    )(cvec, mod_w, mod_b.reshape(depth, 1, n))


def _pair_rms(z, gain, lo):
    z2 = z * z
    s_lo = jnp.sum(jnp.where(lo, z2, 0.0), axis=-1, keepdims=True)
    s_hi = jnp.sum(jnp.where(lo, 0.0, z2), axis=-1, keepdims=True)
    r = jnp.where(lo, lax.rsqrt(s_lo * (1.0 / HEAD_DIM) + EPS), lax.rsqrt(s_hi * (1.0 / HEAD_DIM) + EPS))
    return z * r * gain


def _pair_rope(z, cos, sin_signed, even_blk):
    nxt = pltpu.roll(z, LANES - 16, 1)
    prv = pltpu.roll(z, 16, 1)
    return z * cos + jnp.where(even_blk, nxt, prv) * sin_signed


def _inproj_kernel(has_prev, rope, kv_per_request, *refs):
    if has_prev:
        f_ref, mprev_ref, refs = refs[0], refs[1], refs[2:]
    x_ref, mod_ref, g_ref, w_ref, gains_ref = refs[:5]
    refs = refs[5:]
    if rope:
        cos_ref, sin_ref, refs = refs[0], refs[1], refs[2:]
    if has_prev:
        x_out, refs = refs[0], refs[1:]
    qa_o, ka_o, va_o, xr_o, gr_o, qc_o, kc_o, vc_o = refs
    d = x_ref.shape[-1]
    lane = lax.broadcasted_iota(I32, (1, LANES), 1)
    lo = lane < HEAD_DIM
    even_blk = (lane // 16) % 2 == 0
    scale = HEAD_DIM ** -0.5
    sh1 = mod_ref[:, 0:d]
    sc1 = mod_ref[:, d:2 * d]

    for sub in range(x_ref.shape[0] // TOKEN_SUBTILE):
        rows = pl.ds(sub * TOKEN_SUBTILE, TOKEN_SUBTILE)
        x = x_ref[rows, :]
        if has_prev:
            x = x + mprev_ref[:, 5 * d:6 * d] * f_ref[rows, :]
            x_out[rows, :] = x
        u = _rms(x, g_ref[...]) * (1.0 + sc1) + sh1
        y = _dot(u.astype(BF16), w_ref[...])

        def heads_out(o_ref, off, n_pairs, gain_row, use_rope, mul, dtype, per_request):
            for p in range(n_pairs):
                z = y[:, off + p * LANES: off + (p + 1) * LANES]
                if gain_row is not None:
                    z = _pair_rms(z, gains_ref[gain_row:gain_row + 1, :], lo)
                if use_rope:
                    z = _pair_rope(z, cos_ref[rows, :], sin_ref[rows, :], even_blk)
                if mul != 1.0:
                    z = z * mul
                for half, zh in enumerate((z[:, :HEAD_DIM], z[:, HEAD_DIM:])):
                    if per_request:
                        o_ref[sub, 2 * p + half] = zh.astype(dtype)
                    else:
                        o_ref[2 * p + half, rows, :] = zh.astype(dtype)

        heads_out(qa_o, OFF_QA, NA_HEADS // 2, 0, False, scale, BF16, False)
        heads_out(ka_o, OFF_KA, NA_HEADS // 2, 1, False, 1.0, F32, kv_per_request)
        heads_out(va_o, OFF_VA, NA_HEADS // 2, None, False, 1.0, F32, kv_per_request)
        xr_o[rows, :] = y[:, OFF_XR:OFF_XR + LRU_WIDTH]
        gr_o[rows, :] = y[:, OFF_GR:OFF_GR + LRU_WIDTH]
        heads_out(qc_o, OFF_QC, GQA_HEADS // 2, 2, rope, scale * LOG2_E, BF16, False)
        heads_out(kc_o, OFF_KC, GQA_KV_HEADS // 2, 3, rope, 1.0, F32, kv_per_request)
        heads_out(vc_o, OFF_VC, GQA_KV_HEADS // 2, None, False, 1.0, F32, kv_per_request)


def _inproj(x, prev, mod, g, w_bf, gains, rope_tabs, t_req, kv_per_request):
    t, d = x.shape
    tm = INPROJ_TILE
    nsub = tm // TOKEN_SUBTILE
    n_req = t // t_req
    tok = pl.BlockSpec((tm, d), lambda i: (i, 0))
    modspec = _mod_spec(mod, max(t_req // tm, 1))
    in_specs = [tok, modspec, _full(g), _full(w_bf), _full(gains)]
    args = [x, mod, g, w_bf, gains]
    rope = rope_tabs is not None
    if rope:
        tpr = t_req // tm
        in_specs += [pl.BlockSpec((tm, LANES), lambda i: (i % tpr, 0))] * 2
        args += list(rope_tabs)
    has_prev = prev is not None
    if has_prev:
        ffn, mod_prev = prev
        in_specs = [tok, modspec] + in_specs
        args = [ffn, mod_prev] + args

    def hm(nh, dtype):
        return (pl.BlockSpec((nh, tm, HEAD_DIM), lambda i: (0, i, 0)),
                jax.ShapeDtypeStruct((nh, t, HEAD_DIM), dtype))

    def kv(nh):
        if not kv_per_request:
            return hm(nh, F32)
        assert t_req == TOKEN_SUBTILE
        return (pl.BlockSpec((nsub, nh, t_req, HEAD_DIM), lambda i: (i, 0, 0, 0)),
                jax.ShapeDtypeStruct((n_req, nh, t_req, HEAD_DIM), F32))

    def tk(width):
        return (pl.BlockSpec((tm, width), lambda i: (i, 0)), jax.ShapeDtypeStruct((t, width), F32))

    outs = [hm(NA_HEADS, BF16), kv(NA_HEADS), kv(NA_HEADS), tk(LRU_WIDTH), tk(LRU_WIDTH),
            hm(GQA_HEADS, BF16), kv(GQA_KV_HEADS), kv(GQA_KV_HEADS)]
    if has_prev:
        outs = [tk(d)] + outs
    res = pl.pallas_call(
        functools.partial(_inproj_kernel, has_prev, rope, kv_per_request),
        grid=(t // tm,),
        in_specs=in_specs,
        out_specs=[o[0] for o in outs],
        out_shape=[o[1] for o in outs],
        compiler_params=_cparams("arbitrary"),
        name="inproj",
    )(*args)
    if has_prev:
        return res[0], res[1:]
    return x, res


def _softmax_pv(parts, exp=jnp.exp):
    m = None
    for s, _ in parts:
        mi = jnp.max(s, axis=-1, keepdims=True)
        m = mi if m is None else jnp.maximum(m, mi)
    den = None
    acc = None
    for s, v in parts:
        p = exp(s - m)
        di = jnp.sum(p, axis=-1, keepdims=True)
        oi = _dot(p.astype(BF16), v)
        den = di if den is None else den + di
        acc = oi if acc is None else acc + oi
    return acc / den


def _ctx_attn_kernel(qa_ref, ka_ref, va_ref, qc_ref, kc_ref, vc_ref, oa_ref, oc_ref):
    for h in range(NA_HEADS):
        k = ka_ref[h].astype(BF16)
        v = va_ref[h].astype(BF16)
        oa_ref[h] = _softmax_pv([(_dot_nt(qa_ref[h], k), v)]).astype(BF16)
    for h in range(GQA_HEADS):
        j = h // GQA_GROUP
        k = kc_ref[j].astype(BF16)
        v = vc_ref[j].astype(BF16)
        oc_ref[h] = _softmax_pv([(_dot_nt(qc_ref[h], k), v)], exp=jnp.exp2).astype(BF16)


def _ctx_attention(qa, ka, va, qc, kc, vc):
    n_req, _, s, _ = ka.shape
    t = qa.shape[1]

    def hm(nh):
        return pl.BlockSpec((nh, s, HEAD_DIM), lambda b: (0, b, 0))

    def pr(nh):
        return pl.BlockSpec((None, nh, s, HEAD_DIM), lambda b: (b, 0, 0, 0))

    return pl.pallas_call(
        _ctx_attn_kernel,
        grid=(n_req,),
        in_specs=[hm(NA_HEADS), pr(NA_HEADS), pr(NA_HEADS), hm(GQA_HEADS), pr(GQA_KV_HEADS), pr(GQA_KV_HEADS)],
        out_specs=[hm(NA_HEADS), hm(GQA_HEADS)],
        out_shape=[jax.ShapeDtypeStruct((NA_HEADS, t, HEAD_DIM), BF16),
                   jax.ShapeDtypeStruct((GQA_HEADS, t, HEAD_DIM), BF16)],
        compiler_params=_cparams("arbitrary"),
        name="ctx_attention",
    )(qa, ka, va, qc, kc, vc)


def _na_lat_kernel(grid_rows, q_ref, k_ref, v_ref, ck_ref, cv_ref, *rest):
    bias_refs, (o_ref, s0, s1) = rest[:NA_BLOCKS_PER_STEP], rest[NA_BLOCKS_PER_STEP:]
    nq = NA_ROW_BLOCK * GRID_W
    nkeys = NA_KEY_ROWS * GRID_W
    sub = nq // NA_Q_SPLIT
    ck = ck_ref[...].astype(BF16)
    cv = cv_ref[...].astype(BF16)
    bufs = (s0, s1)
    chains = [(blk, part) for blk in range(NA_BLOCKS_PER_STEP) for part in range(NA_Q_SPLIT)]

    def window(blk):
        rb = pl.program_id(2) * NA_BLOCKS_PER_STEP + blk
        w0 = jnp.clip(NA_ROW_BLOCK * rb - NA_WIN_H // 2, 0, grid_rows - NA_KEY_ROWS)
        return pl.ds(pl.multiple_of(w0 * GRID_W, 4 * GRID_W), nkeys)

    def scores(n):
        blk, part = chains[n]
        rows = pl.ds(blk * nq + part * sub, sub)
        q = q_ref[rows, :]
        bias_ref = bias_refs[blk]
        buf = bufs[n % 2]
        buf[:, :nkeys] = _dot_nt(q, k_ref[window(blk), :].astype(BF16)) + bias_ref[pl.ds(part * sub, sub), :]
        buf[:, nkeys:] = _dot_nt(q, ck)

    def finish(n):
        blk, part = chains[n]
        s = bufs[n % 2][...]
        m = jnp.max(s, axis=-1, keepdims=True)
        p = jnp.exp(s - m)
        den = jnp.sum(p, axis=-1, keepdims=True)
        pb = p.astype(BF16)
        o = _dot(pb[:, :nkeys], v_ref[window(blk), :].astype(BF16)) + _dot(pb[:, nkeys:], cv)
        o_ref[pl.ds(blk * nq + part * sub, sub), :] = (o / den).astype(BF16)

    scores(0)
    for n in range(len(chains)):
        if n + 1 < len(chains):
            scores(n + 1)
        finish(n)


def _na_bias_table(rpb, grid_rows):
    nrb = grid_rows // NA_ROW_BLOCK
    nh, n_dr, n_dc = rpb.shape
    c = jnp.arange(GRID_W)[:, None]
    kc = jnp.arange(GRID_W)[None, :]
    cs = jnp.clip(c - NA_WIN_W // 2, 0, GRID_W - NA_WIN_W)
    in_win = (kc >= cs) & (kc < cs + NA_WIN_W)
    pick = (jnp.arange(n_dc)[:, None, None] == (kc - c + NA_WIN_W - 1)[None]).astype(F32)
    toep = jnp.dot(rpb.astype(F32).reshape(nh * n_dr, n_dc), pick.reshape(n_dc, GRID_W * GRID_W),
                   precision=lax.Precision.HIGHEST).reshape(nh, n_dr, GRID_W, GRID_W)
    toep = jnp.where(in_win, toep, MASK_VALUE)

    plan = []
    for rb in (0, min(1, nrb - 1), nrb - 1):
        r0 = NA_ROW_BLOCK * rb
        w0 = min(max(r0 - NA_WIN_H // 2, 0), grid_rows - NA_KEY_ROWS)
        rows = []
        for r in range(r0, r0 + NA_ROW_BLOCK):
            rs = min(max(r - NA_WIN_H // 2, 0), grid_rows - NA_WIN_H)
            rows.append([kr - r + NA_WIN_H - 1 if rs <= kr < rs + NA_WIN_H else None
                         for kr in range(w0, w0 + NA_KEY_ROWS)])
        plan.append(rows)

    def assemble(toep_ref, o_ref):
        masked = jnp.full((GRID_W, GRID_W), MASK_VALUE, F32)
        for v, rows in enumerate(plan):
            for i, row in enumerate(rows):
                for j, dr in enumerate(row):
                    o_ref[v, i * GRID_W:(i + 1) * GRID_W, j * GRID_W:(j + 1) * GRID_W] = (
                        masked if dr is None else toep_ref[dr])

    nq, nk = NA_ROW_BLOCK * GRID_W, NA_KEY_ROWS * GRID_W
    return pl.pallas_call(
        assemble,
        grid=(nh,),
        in_specs=[pl.BlockSpec((None, n_dr, GRID_W, GRID_W), lambda h: (h, 0, 0, 0))],
        out_specs=pl.BlockSpec((None, len(plan), nq, nk), lambda h: (h, 0, 0, 0)),
        out_shape=jax.ShapeDtypeStruct((nh, len(plan), nq, nk), F32),
        compiler_params=_cparams("arbitrary"),
        name="na_bias_table",
    )(toep)


def _na_lat_attention(qa, ka, va, cache_k, cache_v, layer, bias, dec_seq):
    t = qa.shape[1]
    dec_batch = t // dec_seq
    grid_rows = dec_seq // GRID_W
    nrb = grid_rows // NA_ROW_BLOCK
    nq = NA_ROW_BLOCK * GRID_W

    def variant(rb):
        return jnp.where(rb == 0, 0, jnp.where(rb == nrb - 1, 2, 1))

    nbs = NA_BLOCKS_PER_STEP
    assert nrb % nbs == 0
    nsteps = nrb // nbs
    qspec = pl.BlockSpec((None, nbs * nq, HEAD_DIM), lambda h, b, i: (h, b * nsteps + i, 0))
    kvspec = pl.BlockSpec((None, dec_seq, HEAD_DIM), lambda h, b, i: (h, b, 0))
    cspec = pl.BlockSpec((None, None, None) + cache_k.shape[3:], lambda h, b, i: (b, layer, h, 0, 0))
    bias_specs = [pl.BlockSpec((None, None) + bias.shape[2:],
                               lambda h, b, i, blk=blk: (h, variant(i * nbs + blk), 0, 0)) for blk in range(nbs)]
    nkeys = NA_KEY_ROWS * GRID_W + cache_k.shape[3]
    return pl.pallas_call(
        functools.partial(_na_lat_kernel, grid_rows),
        grid=(NA_HEADS, dec_batch, nsteps),
        in_specs=[qspec, kvspec, kvspec, cspec, cspec] + bias_specs,
        out_specs=qspec,
        out_shape=jax.ShapeDtypeStruct((NA_HEADS, t, HEAD_DIM), BF16),
        scratch_shapes=[pltpu.VMEM((nq // NA_Q_SPLIT, nkeys), F32)] * 2,
        compiler_params=_cparams("arbitrary", "arbitrary", "arbitrary"),
        name="na_lat_attention",
    )(qa, ka, va, cache_k, cache_v, *([bias] * nbs))


def _gqa_lat_kernel(q_ref, k_ref, v_ref, ck_ref, cv_ref, o_ref, *scratch):
    _, tq, hd = q_ref.shape
    g = GQA_GROUP
    nh = GQA_KV_HEADS
    ch = GQA_KEY_CHUNK
    nc = k_ref.shape[1] // ch
    s_bufs = [scratch[2 * j:2 * j + 2] for j in range(nh)]
    qs = [q_ref[j * g:(j + 1) * g].reshape(g * tq, hd) for j in range(nh)]

    tn = (((0,), (0,)), ((), ()))

    def scores(j, c):
        return _dot_nt(k_ref[j, pl.ds(c * ch, ch), :].astype(BF16), qs[j])

    def with_ones(v):
        return jnp.concatenate([v, jnp.ones((v.shape[0], GQA_DEN_COLS), v.dtype)], axis=-1).astype(BF16)

    def update(j, state, s, v):
        m, acc = state
        m_new = jnp.maximum(m, jnp.max(s, axis=0, keepdims=True))
        p = jnp.exp2(s - m_new).astype(BF16)
        acc = jnp.exp2(m - m_new) * acc + lax.dot_general(with_ones(v), p, tn, preferred_element_type=F32)
        return m_new, acc

    states = []
    for j in range(nh):
        s = _dot_nt(ck_ref[j].astype(BF16), qs[j])
        m = jnp.max(s, axis=0, keepdims=True)
        p = jnp.exp2(s - m).astype(BF16)
        states.append((m, lax.dot_general(with_ones(cv_ref[j]), p, tn, preferred_element_type=F32)))
        s_bufs[j][0][...] = scores(j, 0)

    for c in range(nc):
        for j in range(nh):
            if c + 1 < nc:
                s_bufs[j][(c + 1) % 2][...] = scores(j, c + 1)
            v = v_ref[j, pl.ds(c * ch, ch), :]
            states[j] = update(j, states[j], s_bufs[j][c % 2][...], v)
    for j in range(nh):
        m, acc = states[j]
        o = (acc[:hd] / acc[hd:hd + 1]).T
        o_ref[j * g:(j + 1) * g] = o.reshape(g, tq, hd).astype(BF16)


def _gqa_lat_attention(qc, kc, vc, cache_k, cache_v, layer, dec_seq):
    t = qc.shape[1]
    dec_batch = t // dec_seq
    tq = GQA_Q_TILE
    nq = dec_seq // tq
    qspec = pl.BlockSpec((GQA_HEADS, tq, HEAD_DIM), lambda b, i: (0, b * nq + i, 0))
    kvspec = pl.BlockSpec((GQA_KV_HEADS, dec_seq, HEAD_DIM), lambda b, i: (0, b, 0))
    cspec = pl.BlockSpec((None, None) + cache_k.shape[2:], lambda b, i: (b, layer, 0, 0, 0))
    return pl.pallas_call(
        _gqa_lat_kernel,
        grid=(dec_batch, nq),
        in_specs=[qspec, kvspec, kvspec, cspec, cspec],
        out_specs=qspec,
        out_shape=jax.ShapeDtypeStruct((GQA_HEADS, t, HEAD_DIM), BF16),
        scratch_shapes=[pltpu.VMEM((GQA_KEY_CHUNK, GQA_GROUP * tq), F32)] * (2 * GQA_KV_HEADS),
        compiler_params=_cparams("arbitrary", "arbitrary"),
        name="gqa_lat_attention",
    )(qc, kc, vc, cache_k, cache_v)


LRU_CHUNK = 256
LRU_PAD = SUBLANES


def _lru_kernel(xr_ref, gr_ref, h0_ref, cw_ref, cb_ref, w_ref, b_ref, lam_ref,
                y_ref, st_ref, xpad, a_f, u_f, a_b, u_b):
    t, w = xr_ref.shape
    nch = t // LRU_CHUNK
    ngrp = t // SUBLANES

    zeros = jnp.zeros((LRU_PAD, w), F32)
    xpad[pl.ds(0, LRU_PAD), :] = zeros
    xpad[pl.ds(LRU_PAD + t, LRU_PAD), :] = zeros
    xpad[pl.ds(LRU_PAD, t), :] = xr_ref[...]

    lam = lam_ref[...]
    decay = -LRU_C * (jnp.maximum(-lam, 0.0) + jnp.log1p(jnp.exp(-jnp.abs(lam))))
    h0 = h0_ref[...]
    row = lax.broadcasted_iota(I32, (LRU_CHUNK, 1), 0)
    srow = lax.broadcasted_iota(I32, (SUBLANES, 1), 0)
    left = CONV_W // 2

    def gates(c, _):
        t0 = pl.multiple_of(c * LRU_CHUNK, LRU_CHUNK)
        cur = xpad[pl.ds(t0 + LRU_PAD, LRU_CHUNK), :]
        before = xpad[pl.ds(t0, LRU_PAD), :]
        after = xpad[pl.ds(t0 + LRU_PAD + LRU_CHUNK, LRU_PAD), :]
        xc = cb_ref[...]
        for j in range(CONV_W):
            s = left - j
            if s > 0:
                sh = pltpu.roll(cur, s, 0)
                head = jnp.where(srow < s, pltpu.roll(before, s, 0), sh[:SUBLANES])
                tap = jnp.concatenate([head, sh[SUBLANES:]], axis=0)
            elif s < 0:
                sh = pltpu.roll(cur, LRU_CHUNK + s, 0)
                tail = jnp.where(srow >= SUBLANES + s, pltpu.roll(after, SUBLANES + s, 0), sh[-SUBLANES:])
                tap = jnp.concatenate([sh[:-SUBLANES], tail], axis=0)
            else:
                tap = cur
            xc = xc + tap * cw_ref[j:j + 1, :]
        z = _dot(xc.astype(BF16), w_ref[...]) + b_ref[...]
        for d, (a_ref, u_ref, edge_chunk, edge_row) in enumerate(
                ((a_f, u_f, 0, 0), (a_b, u_b, nch - 1, LRU_CHUNK - 1))):
            r = jax.nn.sigmoid(z[:, (2 * d) * w:(2 * d + 1) * w])
            i = jax.nn.sigmoid(z[:, (2 * d + 1) * w:(2 * d + 2) * w])
            log_a = decay[d:d + 1, :] * r
            a = jnp.exp(log_a)
            u = jnp.sqrt(-jnp.tanh(log_a) * (a * a + 1.0)) * (i * xc)
            first = (row == edge_row) & (c == edge_chunk)
            u = u + jnp.where(first, a * h0[d:d + 1, :], 0.0)
            a_ref[pl.ds(t0, LRU_CHUNK), :] = a
            u_ref[pl.ds(t0, LRU_CHUNK), :] = u
        return 0

    lax.fori_loop(0, nch, gates, 0)

    def scan(i, carry):
        hf, hb = carry
        o = pl.multiple_of(i * SUBLANES, SUBLANES)
        a = a_f[pl.ds(o, SUBLANES), :]
        u = u_f[pl.ds(o, SUBLANES), :]
        for s in (1, 2, 4):
            keep = srow >= s
            a_s = jnp.where(keep, pltpu.roll(a, s, 0), 1.0)
            u_s = jnp.where(keep, pltpu.roll(u, s, 0), 0.0)
            u = a * u_s + u
            a = a * a_s
        h = u + a * hf
        u_f[pl.ds(o, SUBLANES), :] = h
        hf = h[SUBLANES - 1:SUBLANES, :]
        o = pl.multiple_of((ngrp - 1 - i) * SUBLANES, SUBLANES)
        a = a_b[pl.ds(o, SUBLANES), :]
        u = u_b[pl.ds(o, SUBLANES), :]
        for s in (1, 2, 4):
            keep = srow < SUBLANES - s
            a_s = jnp.where(keep, pltpu.roll(a, SUBLANES - s, 0), 1.0)
            u_s = jnp.where(keep, pltpu.roll(u, SUBLANES - s, 0), 0.0)
            u = a * u_s + u
            a = a * a_s
        h = u + a * hb
        u_b[pl.ds(o, SUBLANES), :] = h
        hb = h[0:1, :]
        return hf, hb

    zero_row = jnp.zeros((1, w), F32)
    hf, hb = lax.fori_loop(0, ngrp, scan, (zero_row, zero_row))
    st_ref[0:1, :] = hf
    st_ref[1:2, :] = hb

    def emit(c, _):
        t0 = pl.multiple_of(c * LRU_CHUNK, LRU_CHUNK)
        hs = u_f[pl.ds(t0, LRU_CHUNK), :] + u_b[pl.ds(t0, LRU_CHUNK), :]
        y = jax.nn.gelu(gr_ref[pl.ds(t0, LRU_CHUNK), :]) * hs
        y_ref[pl.ds(t0, LRU_CHUNK), :] = y.astype(y_ref.dtype)
        return 0

    lax.fori_loop(0, nch, emit, 0)


def _lru(xr, gr, h0, cw, cb, w_bf, b_cat, lam, t_req):
    t, w = xr.shape
    n_req = t // t_req
    tok = pl.BlockSpec((t_req, w), lambda b: (b, 0))
    st = pl.BlockSpec((None, 2, w), lambda b: (b, 0, 0))
    return pl.pallas_call(
        _lru_kernel,
        grid=(n_req,),
        in_specs=[tok, tok, st, _full(cw), _full(cb), _full(w_bf), _full(b_cat), _full(lam)],
        out_specs=[tok, st],
        out_shape=[jax.ShapeDtypeStruct((t, w), BF16), jax.ShapeDtypeStruct((n_req, 2, w), F32)],
        scratch_shapes=[pltpu.VMEM((t_req + 2 * LRU_PAD, w), F32)] + [pltpu.VMEM((t_req, w), F32)] * 4,
        compiler_params=_cparams("arbitrary"),
        name="rglru",
    )(xr, gr, h0, cw, cb, w_bf, b_cat, lam)


def _outproj_kernel(x_ref, oa_ref, ob_ref, oc_ref, mod_ref, w_ref, gf_ref, rw_ref,
                    xm_ref, u_ref, aff_ref):
    d = x_ref.shape[-1]
    g1 = mod_ref[:, 2 * d:3 * d]
    sh2 = mod_ref[:, 3 * d:4 * d]
    sc2 = mod_ref[:, 4 * d:5 * d]
    rw = rw_ref[...]
    r_hi = rw.astype(BF16)
    r_lo = (rw - r_hi.astype(F32)).astype(BF16)
    for sub in range(x_ref.shape[0] // TOKEN_SUBTILE):
        rows = pl.ds(sub * TOKEN_SUBTILE, TOKEN_SUBTILE)
        pieces = ([oa_ref[h, rows, :].astype(F32) for h in range(NA_HEADS)] + [ob_ref[rows, :].astype(F32)]
                  + [oc_ref[h, rows, :].astype(F32) for h in range(GQA_HEADS)])
        o = jnp.concatenate(pieces, axis=-1).astype(BF16)
        mix = _dot(o, w_ref[...])
        xm = x_ref[rows, :] + g1 * mix
        xm_ref[rows, :] = xm
        u = _rms(xm, gf_ref[...]) * (1.0 + sc2) + sh2
        u_ref[rows, :] = u
        u_hi = u.astype(BF16)
        u_lo = (u - u_hi.astype(F32)).astype(BF16)
        lg = _dot_nt(r_hi, u_hi) + (_dot_nt(r_lo, u_hi) + _dot_nt(r_hi, u_lo))
        m = jnp.max(lg, axis=0, keepdims=True)
        e = jnp.exp(lg - m)
        aff_ref[:, rows] = e / jnp.sum(e, axis=0, keepdims=True)


def _outproj(x, oa, ob, oc, mod, w_bf, gf, rw_t, t_req):
    t, d = x.shape
    tm = TOKEN_TILE
    tpr = max(t_req // tm, 1)
    tok = pl.BlockSpec((tm, d), lambda i: (i, 0))
    hm = pl.BlockSpec((NA_HEADS, tm, HEAD_DIM), lambda i: (0, i, 0))
    return pl.pallas_call(
        _outproj_kernel,
        grid=(t // tm,),
        in_specs=[tok, hm, pl.BlockSpec((tm, LRU_WIDTH), lambda i: (i, 0)), hm,
                  _mod_spec(mod, tpr), _full(w_bf), _full(gf), _full(rw_t)],
        out_specs=[tok, tok, pl.BlockSpec((N_EXPERTS, tm), lambda i: (0, i))],
        out_shape=[jax.ShapeDtypeStruct((t, d), F32), jax.ShapeDtypeStruct((t, d), F32),
                   jax.ShapeDtypeStruct((N_EXPERTS, t), F32)],
        compiler_params=_cparams("arbitrary"),
        name="outproj_router",
    )(x, oa, ob, oc, mod, w_bf, gf, rw_t)


PREFIX_BLOCK = 256
TOPK_REQUESTS_PER_STEP = 8


def _topk_kernel(cap, t_req, compact, aff_ref, *outs):
    ne, tt = aff_ref.shape
    ng = tt // t_req
    slot_ref = outs[0]
    keys = [pltpu.bitcast(aff_ref[:, g * t_req:(g + 1) * t_req], I32) for g in range(ng)]

    def search(it, thrs):
        bit = jnp.left_shift(jnp.int32(1), 30 - it)
        out = []
        for g in range(ng):
            cand = thrs[g] | bit
            cnt = jnp.sum((keys[g] >= cand).astype(F32), axis=-1, keepdims=True)
            out.append(jnp.where(cnt >= cap, cand, thrs[g]))
        return tuple(out)

    thrs = lax.fori_loop(0, 31, search, tuple(jnp.zeros((ne, 1), I32) for _ in range(ng)))

    ri = lax.broadcasted_iota(I32, (PREFIX_BLOCK, PREFIX_BLOCK), 0)
    ci = lax.broadcasted_iota(I32, (PREFIX_BLOCK, PREFIX_BLOCK), 1)
    tri = jnp.where(ri <= ci, 1.0, 0.0).astype(BF16)

    def prefix(mask_f32):
        carry = jnp.zeros((ne, 1), F32)
        blocks = []
        for b in range(t_req // PREFIX_BLOCK):
            blk = mask_f32[:, b * PREFIX_BLOCK:(b + 1) * PREFIX_BLOCK]
            blocks.append(_dot(blk.astype(BF16), tri) + carry)
            carry = carry + jnp.sum(blk, axis=-1, keepdims=True)
        return blocks[0] if len(blocks) == 1 else jnp.concatenate(blocks, axis=-1)

    for g in range(ng):
        gt = keys[g] > thrs[g]
        eq = keys[g] == thrs[g]
        need = cap - jnp.sum(gt.astype(F32), axis=-1, keepdims=True)
        tie_rank = prefix(eq.astype(F32))
        sel = jnp.where(gt, 1.0, jnp.where(eq, (tie_rank <= need).astype(F32), 0.0))
        slot = jnp.where(sel > 0.0, prefix(sel), 0.0)
        slot_ref[:, g * t_req:(g + 1) * t_req] = slot

    if not compact:
        return
    assert ng == 1
    idx_ref, gate_ref = outs[1], outs[2]
    tok = lax.broadcasted_iota(I32, (ne, t_req), 1)
    valid = (slot > 0.0).astype(I32)
    dist = jnp.where(slot > 0.0, tok - (slot.astype(I32) - 1), 0)
    tokv = tok
    gate = aff_ref[...]
    for k in range((t_req - 1).bit_length()):
        sh = 1 << k
        nb = lambda x: pltpu.roll(x, t_req - sh, 1)
        n_valid, n_dist = nb(valid), nb(dist)
        incoming = (n_valid * ((n_dist >> k) & 1)) > 0
        stay = valid * (1 - ((dist >> k) & 1))
        tokv = jnp.where(incoming, nb(tokv), tokv)
        gate = jnp.where(incoming, nb(gate), gate)
        dist = jnp.where(incoming, n_dist, dist)
        valid = jnp.where(incoming, 1, stay)
    idx_ref[...] = tokv[:, :cap]
    gate_ref[...] = gate[:, :cap]


def _topk(aff, t_req, cap, compact):
    ne, t = aff.shape
    n_req = t // t_req
    ng = 1 if compact else min(TOPK_REQUESTS_PER_STEP, n_req)
    assert n_req % ng == 0
    lanes = ng * t_req
    out_specs = [pl.BlockSpec((ne, lanes), lambda b: (0, b))]
    out_shape = [jax.ShapeDtypeStruct((ne, t), F32)]
    if compact:
        out_specs += [pl.BlockSpec((None, ne, cap), lambda b: (b, 0, 0))] * 2
        out_shape += [jax.ShapeDtypeStruct((n_req, ne, cap), I32), jax.ShapeDtypeStruct((n_req, ne, cap), F32)]
    return pl.pallas_call(
        functools.partial(_topk_kernel, cap, t_req, compact),
        grid=(n_req // ng,),
        in_specs=[pl.BlockSpec((ne, lanes), lambda b: (0, b))],
        out_specs=out_specs,
        out_shape=out_shape,
        compiler_params=_cparams("arbitrary"),
        name="expert_topk",
    )(aff)


def _slot_onehot(slot_ref, cap):
    ne = slot_ref.shape[0]
    want = (lax.broadcasted_iota(I32, (cap, 1), 0) + 1).astype(F32)
    return jnp.concatenate([jnp.where(slot_ref[e:e + 1, :] == want, 1.0, 0.0) for e in range(ne)], axis=0)


def _ctx_dispatch_kernel(slot_ref, u_ref, x_ref):
    ne, cap, d = x_ref.shape
    onehot = _slot_onehot(slot_ref, cap).astype(BF16)
    x = _dot(onehot, u_ref[...].astype(BF16))
    x_ref[...] = x.reshape(ne, cap, d).astype(BF16)


def _ctx_dispatch(slot, u, t_req, cap):
    ne, t = slot.shape
    n_req = t // t_req
    d = u.shape[-1]
    return pl.pallas_call(
        _ctx_dispatch_kernel,
        grid=(n_req,),
        in_specs=[pl.BlockSpec((ne, t_req), lambda b: (0, b)),
                  pl.BlockSpec((t_req, d), lambda b: (b, 0))],
        out_specs=pl.BlockSpec((ne, None, cap, d), lambda b: (0, b, 0, 0)),
        out_shape=jax.ShapeDtypeStruct((ne, n_req, cap, d), BF16),
        compiler_params=_cparams("arbitrary"),
        name="ctx_dispatch",
    )(slot, u)


def _ctx_combine_kernel(slot_ref, aff_ref, y_ref, o_ref):
    ne, cap, d = y_ref.shape
    want = (lax.broadcasted_iota(I32, (cap, 1), 0) + 1).astype(F32)
    hots, gates = [], []
    for e in range(ne):
        hot = jnp.where(slot_ref[e:e + 1, :] == want, 1.0, 0.0)
        hots.append(hot)
        gates.append(jnp.sum(hot * aff_ref[e:e + 1, :], axis=-1, keepdims=True))
    onehot = jnp.concatenate(hots, axis=0).astype(BF16)
    y = y_ref[...].reshape(ne * cap, d) * jnp.concatenate(gates, axis=0)
    y_hi = y.astype(BF16)
    y_lo = (y - y_hi.astype(F32)).astype(BF16)
    tn = (((0,), (0,)), ((), ()))
    o_ref[...] = (lax.dot_general(onehot, y_hi, tn, preferred_element_type=F32)
                  + lax.dot_general(onehot, y_lo, tn, preferred_element_type=F32))


def _ctx_combine(slot, aff, ye, t_req):
    ne, n_req, cap, d = ye.shape
    lane = pl.BlockSpec((ne, t_req), lambda b: (0, b))
    return pl.pallas_call(
        _ctx_combine_kernel,
        grid=(n_req,),
        in_specs=[lane, lane, pl.BlockSpec((ne, None, cap, d), lambda b: (0, b, 0, 0))],
        out_specs=pl.BlockSpec((t_req, d), lambda b: (b, 0)),
        out_shape=jax.ShapeDtypeStruct((n_req * t_req, d), F32),
        compiler_params=_cparams("arbitrary"),
        name="ctx_combine",
    )(slot, aff, ye)


SC_GATHER_ROWS = 64


def _lat_dispatch(gidx, u):
    ne, rows = gidx.shape
    d = u.shape[-1]
    total = ne * rows
    mesh = plsc.VectorSubcoreMesh(core_axis_name="c", subcore_axis_name="s")
    n_cores = mesh.num_cores
    n_workers = n_cores * mesh.num_subcores
    chunk = SC_GATHER_ROWS
    per_worker = total // n_workers
    assert total % n_workers == 0 and per_worker % chunk == 0

    @functools.partial(
        pl.kernel, mesh=mesh, out_type=jax.ShapeDtypeStruct((total, d), u.dtype),
        scratch_types=[pltpu.VMEM((chunk,), I32), pltpu.VMEM((chunk, d), u.dtype), pltpu.SemaphoreType.DMA])
    def gather(u_hbm, idx_hbm, out_hbm, idx_v, rows_v, sem):
        worker = lax.axis_index("s") * n_cores + lax.axis_index("c")
        base = worker * per_worker

        @pl.loop(0, per_worker // chunk)
        def _(i):
            off = pl.multiple_of(base + i * chunk, SUBLANES)
            pltpu.sync_copy(idx_hbm.at[pl.ds(off, chunk)], idx_v)
            pltpu.async_copy(u_hbm.at[idx_v], rows_v, sem).wait()
            pltpu.sync_copy(rows_v, out_hbm.at[pl.ds(off, chunk)])

    return gather(u, gidx.reshape(total)).reshape(ne, rows, d)


def _ffn_kernel(nt_ctx, xc_ref, xl_ref, wg_ref, wu_ref, wd_ref, yc_ref, yl_ref, wg_bf, wu_bf, wd_bf):
    j = pl.program_id(1)

    @pl.when(j == 0)
    def _():
        wg_bf[...] = wg_ref[...].astype(BF16)
        wu_bf[...] = wu_ref[...].astype(BF16)
        wd_bf[...] = wd_ref[...].astype(BF16)

    def ffn(x):
        h = jax.nn.silu(_dot(x, wg_bf[...])) * _dot(x, wu_bf[...])
        return _dot(h.astype(BF16), wd_bf[...])

    @pl.when(j < nt_ctx)
    def _():
        yc_ref[...] = ffn(xc_ref[...])

    @pl.when(j >= nt_ctx)
    def _():
        yl_ref[...] = ffn(xl_ref[...].astype(BF16))


def _expert_ffn(x_ctx, x_lat, wg, wu, wd, layer):
    ne, rows_ctx, d = x_ctx.shape
    rows_lat = x_lat.shape[1]
    ff = wg.shape[-1]
    tr = FFN_ROW_TILE
    nt_ctx, nt_lat = rows_ctx // tr, rows_lat // tr
    wspec = lambda a: pl.BlockSpec((None, None) + a.shape[2:], lambda e, j: (layer, e, 0, 0))
    ctx_spec = pl.BlockSpec((None, tr, d), lambda e, j: (e, jnp.minimum(j, nt_ctx - 1), 0))
    lat_spec = pl.BlockSpec((None, tr, d), lambda e, j: (e, jnp.maximum(j - nt_ctx, 0), 0))
    return pl.pallas_call(
        functools.partial(_ffn_kernel, nt_ctx),
        grid=(ne, nt_ctx + nt_lat),
        in_specs=[ctx_spec, lat_spec, wspec(wg), wspec(wu), wspec(wd)],
        out_specs=[ctx_spec, lat_spec],
        out_shape=[jax.ShapeDtypeStruct((ne, rows_ctx, d), F32), jax.ShapeDtypeStruct((ne, rows_lat, d), F32)],
        scratch_shapes=[pltpu.VMEM((d, ff), BF16), pltpu.VMEM((d, ff), BF16), pltpu.VMEM((ff, d), BF16)],
        compiler_params=_cparams("arbitrary", "arbitrary"),
        name="expert_ffn",
    )(x_ctx, x_lat, wg, wu, wd)


COMBINE_GROUP = 4


def _lat_combine_kernel(idx_ref, gate_ref, y_ref, o_ref):
    e = pl.program_id(1)
    cap = y_ref.shape[0]

    @pl.when(e == 0)
    def _():
        o_ref[...] = jnp.zeros(o_ref.shape, o_ref.dtype)

    def body(q, _):
        r0 = q * COMBINE_GROUP
        toks = [idx_ref[0, r0 + k] for k in range(COMBINE_GROUP)]
        acc = [o_ref[pl.ds(i, 1), :] for i in toks]
        for k in range(COMBINE_GROUP):
            o_ref[pl.ds(toks[k], 1), :] = acc[k] + gate_ref[0, r0 + k] * y_ref[pl.ds(r0 + k, 1), :]
        return 0

    lax.fori_loop(0, cap // COMBINE_GROUP, body, 0)


def _lat_combine(idx, gate, ye, t_req):
    n_req, ne, _, cap = idx.shape
    d = ye.shape[-1]
    sspec = pl.BlockSpec((None, None, 1, cap), lambda b, e: (b, e, 0, 0), memory_space=pltpu.SMEM)
    return pl.pallas_call(
        _lat_combine_kernel,
        grid=(n_req, ne),
        in_specs=[sspec, sspec, pl.BlockSpec((None, cap, d), lambda b, e: (e, b, 0))],
        out_specs=pl.BlockSpec((t_req, d), lambda b, e: (b, 0)),
        out_shape=jax.ShapeDtypeStruct((n_req * t_req, d), F32),
        compiler_params=_cparams("arbitrary", "arbitrary"),
        name="lat_combine",
    )(idx, gate, ye)


def _final_kernel(xm_ref, f_ref, mod_ref, o_ref):
    d = xm_ref.shape[-1]
    o_ref[...] = xm_ref[...] + mod_ref[:, 5 * d:6 * d] * f_ref[...]


def _final(xm, ffn, mod, t_req):
    t, d = xm.shape
    tm = TOKEN_TILE
    tpr = max(t_req // tm, 1)
    tok = pl.BlockSpec((tm, d), lambda i: (i, 0))
    return pl.pallas_call(
        _final_kernel,
        grid=(t // tm,),
        in_specs=[tok, tok, _mod_spec(mod, tpr)],
        out_specs=tok,
        out_shape=jax.ShapeDtypeStruct((t, d), F32),
        compiler_params=_cparams("arbitrary"),
        name="final_residual",
    )(xm, ffn, mod)


def _rope_tables(dec_seq):
    pos = jnp.arange(dec_seq)
    n = HEAD_DIM // 4
    inv = ROPE_THETA ** (-jnp.arange(n, dtype=F32) / n)
    ang_r = (pos // GRID_W).astype(F32)[:, None] * inv[None, :]
    ang_c = (pos % GRID_W).astype(F32)[:, None] * inv[None, :]
    cr, sr, cc, sc = jnp.cos(ang_r), jnp.sin(ang_r), jnp.cos(ang_c), jnp.sin(ang_c)
    cos = jnp.concatenate([cr, cr, cc, cc] * 2, axis=-1)
    sin = jnp.concatenate([-sr, sr, -sc, sc] * 2, axis=-1)
    return cos, sin


def _block_diag(wts):
    nb, bw, _ = wts.shape
    eye = jnp.eye(nb, dtype=wts.dtype)
    return (wts[:, :, None, :] * eye[:, None, :, None]).reshape(nb * bw, nb * bw)


def kernel(x_prompt, x_sample, c, cache_na_k, cache_na_v, cache_gqa_k, cache_gqa_v, state_lru, c_ctx, mod_w, mod_b, norm_mix_g, norm_ffn_g, w_in, na_q_norm_g, na_k_norm_g, na_rpb, conv_w, conv_b, lru_w_a, lru_b_a, lru_w_i, lru_b_i, lru_lambda, gqa_q_norm_g, gqa_k_norm_g, w_out, router_w, expert_w_gate, expert_w_up, expert_w_down):
    batch, seq, d = x_prompt.shape
    dec_batch, dec_seq, _ = x_sample.shape
    depth = mod_w.shape[0]
    t_ctx = batch * seq
    t_lat = dec_batch * dec_seq
    assert seq == TOKEN_SUBTILE and dec_seq % TOKEN_TILE == 0 and (batch * seq) % TOKEN_TILE == 0
    assert (dec_seq // GRID_W) % NA_ROW_BLOCK == 0 and dec_seq // GRID_W >= NA_KEY_ROWS
    cap_ctx = EC_FACTOR * seq // N_EXPERTS
    cap_lat = EC_FACTOR * dec_seq // N_EXPERTS
    rows_ctx = batch * cap_ctx
    rows_lat = dec_batch * cap_lat
    assert rows_ctx % FFN_ROW_TILE == 0 and rows_lat % FFN_ROW_TILE == 0 and cap_lat % COMBINE_GROUP == 0

    n_mod_rows = SUBLANES
    assert 1 + dec_batch <= n_mod_rows
    cvec = jnp.zeros((n_mod_rows, d), F32).at[0].set(c_ctx).at[1:1 + dec_batch].set(c)
    mods = _modulation(cvec, mod_w, mod_b)

    rope_tabs = _rope_tables(dec_seq)
    x_c = x_prompt.reshape(t_ctx, d)
    x_l = x_sample.reshape(t_lat, d)
    zeros_h0 = jnp.zeros((batch, 2, LRU_WIDTH), F32)

    new_na_k, new_na_v, new_gqa_k, new_gqa_v, new_lru = [], [], [], [], []
    prev_c = prev_l = None
    for l in range(depth):
        mod_c = mods[l, 0:1].reshape(1, 1, 6 * d)
        mod_l = mods[l, 1:1 + dec_batch].reshape(dec_batch, 1, 6 * d)
        tile2 = lambda g: jnp.concatenate([g, g])
        gains = jnp.stack([tile2(na_q_norm_g[l]), tile2(na_k_norm_g[l]),
                           tile2(gqa_q_norm_g[l]), tile2(gqa_k_norm_g[l])])
        w_in_bf = w_in[l].astype(BF16)
        g_mix = norm_mix_g[l][None, :]
        x_c, (qa_c, ka_c, va_c, xr_c, gr_c, qc_c, kc_c, vc_c) = _inproj(
            x_c, prev_c, mod_c, g_mix, w_in_bf, gains, None, seq, True)
        x_l, (qa_l, ka_l, va_l, xr_l, gr_l, qc_l, kc_l, vc_l) = _inproj(
            x_l, prev_l, mod_l, g_mix, w_in_bf, gains, rope_tabs, dec_seq, False)

        oa_c, oc_c = _ctx_attention(qa_c, ka_c, va_c, qc_c, kc_c, vc_c)
        bias = _na_bias_table(na_rpb[l], dec_seq // GRID_W)
        oa_l = _na_lat_attention(qa_l, ka_l, va_l, cache_na_k, cache_na_v, l, bias, dec_seq)
        oc_l = _gqa_lat_attention(qc_l, kc_l, vc_l, cache_gqa_k, cache_gqa_v, l, dec_seq)

        w_gate = jnp.concatenate([_block_diag(lru_w_a[l, 0]), _block_diag(lru_w_i[l, 0]),
                                  _block_diag(lru_w_a[l, 1]), _block_diag(lru_w_i[l, 1])], axis=1).astype(BF16)
        b_gate = jnp.concatenate([lru_b_a[l, 0], lru_b_i[l, 0], lru_b_a[l, 1], lru_b_i[l, 1]])[None, :]
        lru_args = (conv_w[l], conv_b[l][None, :], w_gate, b_gate, lru_lambda[l])
        ob_c, st = _lru(xr_c, gr_c, zeros_h0, *lru_args, seq)
        ob_l, _ = _lru(xr_l, gr_l, state_lru[:, l], *lru_args, dec_seq)

        w_out_bf = w_out[l].astype(BF16)
        g_ffn = norm_ffn_g[l][None, :]
        rw_t = router_w[l].T
        xm_c, u_c, aff_c = _outproj(x_c, oa_c, ob_c, oc_c, mod_c, w_out_bf, g_ffn, rw_t, seq)
        xm_l, u_l, aff_l = _outproj(x_l, oa_l, ob_l, oc_l, mod_l, w_out_bf, g_ffn, rw_t, dec_seq)

        (slot_c,) = _topk(aff_c, seq, cap_ctx, False)
        _, idx_l, gate_l = _topk(aff_l, dec_seq, cap_lat, True)
        g_l = idx_l + (jnp.arange(dec_batch, dtype=I32) * dec_seq)[:, None, None]
        gidx_l = g_l.transpose(1, 0, 2).reshape(N_EXPERTS, rows_lat)
        idx_l = idx_l.reshape(dec_batch, N_EXPERTS, 1, cap_lat)
        gate_l = gate_l.reshape(dec_batch, N_EXPERTS, 1, cap_lat)

        xe_c = _ctx_dispatch(slot_c, u_c, seq, cap_ctx).reshape(N_EXPERTS, rows_ctx, d)
        xe_l = _lat_dispatch(gidx_l, u_l)
        ye_c, ye_l = _expert_ffn(xe_c, xe_l, expert_w_gate, expert_w_up, expert_w_down, l)
        ffn_c = _ctx_combine(slot_c, aff_c, ye_c.reshape(N_EXPERTS, batch, cap_ctx, d), seq)
        ffn_l = _lat_combine(idx_l, gate_l, ye_l, dec_seq)
        prev_c, prev_l = (ffn_c, mod_c), (ffn_l, mod_l)
        x_c, x_l = xm_c, xm_l

        new_na_k.append(ka_c)
        new_na_v.append(va_c)
        new_gqa_k.append(kc_c)
        new_gqa_v.append(vc_c)
        new_lru.append(st)

    y_p = _final(x_c, prev_c[0], prev_c[1], seq)
    y_s = _final(x_l, prev_l[0], prev_l[1], dec_seq)
    return (y_p.reshape(batch, seq, d), y_s.reshape(dec_batch, dec_seq, d),
            jnp.stack(new_na_k, axis=1), jnp.stack(new_na_v, axis=1),
            jnp.stack(new_gqa_k, axis=1), jnp.stack(new_gqa_v, axis=1),
            jnp.stack(new_lru, axis=1))
```

```python
import functools

import jax
import jax.numpy as jnp
from jax import lax
from jax.experimental import pallas as pl
from jax.experimental.pallas import tpu as pltpu
from jax.experimental.pallas import tpu_sc as plsc

F32 = jnp.float32
BF16 = jnp.bfloat16
I32 = jnp.int32

HEAD_DIM = 64
NA_HEADS = 6
GQA_HEADS = 6
GQA_KV_HEADS = 2
GQA_GROUP = GQA_HEADS // GQA_KV_HEADS
LRU_WIDTH = 256
LRU_C = 8.0
CONV_W = 4
GRID_W = 64
NA_WIN_H = 8
NA_WIN_W = 16
N_EXPERTS = 16
EC_FACTOR = 2
ROPE_THETA = 10000.0
EPS = 1e-6
LOG2_E = 1.4426950408889634
NA_DIM = NA_HEADS * HEAD_DIM
GQA_Q_DIM = GQA_HEADS * HEAD_DIM
GQA_KV_DIM = GQA_KV_HEADS * HEAD_DIM
OFF_QA = 0
OFF_KA = OFF_QA + NA_DIM
OFF_VA = OFF_KA + NA_DIM
OFF_XR = OFF_VA + NA_DIM
OFF_GR = OFF_XR + LRU_WIDTH
OFF_QC = OFF_GR + LRU_WIDTH
OFF_KC = OFF_QC + GQA_Q_DIM
OFF_VC = OFF_KC + GQA_KV_DIM

LANES = 128
SUBLANES = 8
VMEM_LIMIT_BYTES = 56 * 1024 * 1024

TOKEN_TILE = 512
TOKEN_SUBTILE = 256
INPROJ_TILE = 256
NA_ROW_BLOCK = 4
NA_KEY_ROWS = NA_ROW_BLOCK + NA_WIN_H
NA_Q_SPLIT = 1
GQA_Q_TILE = 256
GQA_KEY_CHUNK = 256
DEN_COLS = 16
FFN_ROW_TILE = 512
MASK_VALUE = -1e30


def _cparams(*sem, flags=None):
    return pltpu.CompilerParams(dimension_semantics=sem, vmem_limit_bytes=VMEM_LIMIT_BYTES, flags=flags)


def _dot(a, b):
    return jnp.dot(a, b, preferred_element_type=F32)


def _dot_nt(a, b):
    return lax.dot_general(a, b, (((1,), (1,)), ((), ())), preferred_element_type=F32)


def _rms(x, g):
    ms = jnp.mean(x * x, axis=-1, keepdims=True)
    return x * lax.rsqrt(ms + EPS) * g


def _full(a):
    nd = a.ndim
    return pl.BlockSpec(a.shape, lambda *_: (0,) * nd)


def _mod_spec(mod, tiles_per_request):
    blk = (None, 1, mod.shape[-1])
    if mod.shape[0] == 1:
        return pl.BlockSpec(blk, lambda i: (0, 0, 0))
    return pl.BlockSpec(blk, lambda i: (i // tiles_per_request, 0, 0))


def _mod_kernel(c_ref, w_ref, b_ref, o_ref):
    s = jax.nn.silu(c_ref[...])
    o_ref[...] = jnp.dot(s, w_ref[...], precision=lax.Precision.HIGHEST,
                         preferred_element_type=F32) + b_ref[...]


def _modulation(cvec, mod_w, mod_b):
    depth, d, n = mod_w.shape
    rows = cvec.shape[0]
    tn = 1536
    return pl.pallas_call(
        _mod_kernel,
        grid=(depth, n // tn),
        in_specs=[
            pl.BlockSpec((rows, d), lambda l, j: (0, 0)),
            pl.BlockSpec((None, d, tn), lambda l, j: (l, 0, j)),
            pl.BlockSpec((None, 1, tn), lambda l, j: (l, 0, j)),
        ],
        out_specs=pl.BlockSpec((None, rows, tn), lambda l, j: (l, 0, j)),
        out_shape=jax.ShapeDtypeStruct((depth, rows, n), F32),
        compiler_params=_cparams("arbitrary", "arbitrary"),
        name="modulation",
    )(cvec, mod_w, mod_b.reshape(depth, 1, n))


def _pair_rms(z, gain, lo):
    z2 = z * z
    s_lo = jnp.sum(jnp.where(lo, z2, 0.0), axis=-1, keepdims=True)
    s_hi = jnp.sum(jnp.where(lo, 0.0, z2), axis=-1, keepdims=True)
    r = jnp.where(lo, lax.rsqrt(s_lo * (1.0 / HEAD_DIM) + EPS), lax.rsqrt(s_hi * (1.0 / HEAD_DIM) + EPS))
    return z * r * gain


def _pair_rope(z, cos, sin_signed, even_blk):
    nxt = pltpu.roll(z, LANES - 16, 1)
    prv = pltpu.roll(z, 16, 1)
    return z * cos + jnp.where(even_blk, nxt, prv) * sin_signed


def _inproj_kernel(has_prev, rope, kv_per_request, n_kv_layers, *refs):
    if has_prev:
        f_ref, mprev_ref, refs = refs[0], refs[1], refs[2:]
    x_ref, mod_ref, g_ref, w_ref, gains_ref = refs[:5]
    refs = refs[5:]
    if rope:
        cos_ref, sin_ref, refs = refs[0], refs[1], refs[2:]
    n_old = 4 * max(n_kv_layers - 1, 0)
    kv_old, refs = refs[:n_old], refs[n_old:]
    if has_prev:
        x_out, refs = refs[0], refs[1:]
    qa_o, ka_o, va_o, xr_o, gr_o, qc_o, kc_o, vc_o = refs
    for kind, o_ref in enumerate((ka_o, va_o, kc_o, vc_o)):
        for l in range(n_kv_layers - 1):
            o_ref[:, l] = kv_old[kind * (n_kv_layers - 1) + l][...]
    d = x_ref.shape[-1]
    lane = lax.broadcasted_iota(I32, (1, LANES), 1)
    lo = lane < HEAD_DIM
    even_blk = (lane // 16) % 2 == 0
    scale = HEAD_DIM ** -0.5
    sh1 = mod_ref[:, 0:d]
    sc1 = mod_ref[:, d:2 * d]

    for sub in range(x_ref.shape[0] // TOKEN_SUBTILE):
        rows = pl.ds(sub * TOKEN_SUBTILE, TOKEN_SUBTILE)
        x = x_ref[rows, :]
        if has_prev:
            x = x + mprev_ref[:, 5 * d:6 * d] * f_ref[rows, :]
            x_out[rows, :] = x
        u = _rms(x, g_ref[...]) * (1.0 + sc1) + sh1
        y = _dot(u.astype(BF16), w_ref[...])

        def heads_out(o_ref, off, n_pairs, gain_row, use_rope, mul, dtype, per_request):
            for p in range(n_pairs):
                z = y[:, off + p * LANES: off + (p + 1) * LANES]
                if gain_row is not None:
                    z = _pair_rms(z, gains_ref[gain_row:gain_row + 1, :], lo)
                if use_rope:
                    z = _pair_rope(z, cos_ref[rows, :], sin_ref[rows, :], even_blk)
                if mul != 1.0:
                    z = z * mul
                for half, zh in enumerate((z[:, :HEAD_DIM], z[:, HEAD_DIM:])):
                    if per_request and n_kv_layers:
                        o_ref[sub, n_kv_layers - 1, 2 * p + half] = zh.astype(dtype)
                    elif per_request:
                        o_ref[sub, 2 * p + half] = zh.astype(dtype)
                    else:
                        o_ref[2 * p + half, rows, :] = zh.astype(dtype)

        heads_out(qa_o, OFF_QA, NA_HEADS // 2, 0, False, scale * LOG2_E, BF16, False)
        heads_out(ka_o, OFF_KA, NA_HEADS // 2, 1, False, 1.0, F32, kv_per_request)
        heads_out(va_o, OFF_VA, NA_HEADS // 2, None, False, 1.0, F32, kv_per_request)
        xr_o[rows, :] = y[:, OFF_XR:OFF_XR + LRU_WIDTH]
        gr_o[rows, :] = y[:, OFF_GR:OFF_GR + LRU_WIDTH]
        heads_out(qc_o, OFF_QC, GQA_HEADS // 2, 2, rope, scale * LOG2_E, BF16, False)
        heads_out(kc_o, OFF_KC, GQA_KV_HEADS // 2, 3, rope, 1.0, F32, kv_per_request)
        heads_out(vc_o, OFF_VC, GQA_KV_HEADS // 2, None, False, 1.0, F32, kv_per_request)


def _inproj(x, prev, mod, g, w_bf, gains, rope_tabs, t_req, kv_per_request, kv_old=None):
    t, d = x.shape
    tm = INPROJ_TILE
    nsub = tm // TOKEN_SUBTILE
    n_req = t // t_req
    tok = pl.BlockSpec((tm, d), lambda i: (i, 0))
    modspec = _mod_spec(mod, max(t_req // tm, 1))
    in_specs = [tok, modspec, _full(g), _full(w_bf), _full(gains)]
    args = [x, mod, g, w_bf, gains]
    rope = rope_tabs is not None
    if rope:
        tpr = t_req // tm
        in_specs += [pl.BlockSpec((tm, LANES), lambda i: (i % tpr, 0))] * 2
        args += list(rope_tabs)
    n_kv_layers = 0
    if kv_old is not None:
        assert kv_per_request
        n_kv_layers = len(kv_old[0]) + 1
        for olds in kv_old:
            for a in olds:
                in_specs.append(pl.BlockSpec((nsub,) + a.shape[1:], lambda i: (i, 0, 0, 0)))
                args.append(a)
    has_prev = prev is not None
    if has_prev:
        ffn, mod_prev = prev
        in_specs = [tok, modspec] + in_specs
        args = [ffn, mod_prev] + args

    def hm(nh, dtype):
        return (pl.BlockSpec((nh, tm, HEAD_DIM), lambda i: (0, i, 0)),
                jax.ShapeDtypeStruct((nh, t, HEAD_DIM), dtype))

    def kv(nh):
        if not kv_per_request:
            return hm(nh, F32)
        assert t_req == TOKEN_SUBTILE
        if n_kv_layers:
            return (pl.BlockSpec((nsub, n_kv_layers, nh, t_req, HEAD_DIM), lambda i: (i, 0, 0, 0, 0)),
                    jax.ShapeDtypeStruct((n_req, n_kv_layers, nh, t_req, HEAD_DIM), F32))
        return (pl.BlockSpec((nsub, nh, t_req, HEAD_DIM), lambda i: (i, 0, 0, 0)),
                jax.ShapeDtypeStruct((n_req, nh, t_req, HEAD_DIM), F32))

    def tk(width):
        return (pl.BlockSpec((tm, width), lambda i: (i, 0)), jax.ShapeDtypeStruct((t, width), F32))

    outs = [hm(NA_HEADS, BF16), kv(NA_HEADS), kv(NA_HEADS), tk(LRU_WIDTH), tk(LRU_WIDTH),
            hm(GQA_HEADS, BF16), kv(GQA_KV_HEADS), kv(GQA_KV_HEADS)]
    if has_prev:
        outs = [tk(d)] + outs
    res = pl.pallas_call(
        functools.partial(_inproj_kernel, has_prev, rope, kv_per_request, n_kv_layers),
        grid=(t // tm,),
        in_specs=in_specs,
        out_specs=[o[0] for o in outs],
        out_shape=[o[1] for o in outs],
        compiler_params=_cparams("arbitrary"),
        name="inproj",
    )(*args)
    if has_prev:
        return res[0], res[1:]
    return x, res


def _softmax_pv(parts, exp=jnp.exp):
    m = None
    for s, _ in parts:
        mi = jnp.max(s, axis=-1, keepdims=True)
        m = mi if m is None else jnp.maximum(m, mi)
    den = None
    acc = None
    for s, v in parts:
        p = exp(s - m)
        di = jnp.sum(p, axis=-1, keepdims=True)
        oi = _dot(p.astype(BF16), v)
        den = di if den is None else den + di
        acc = oi if acc is None else acc + oi
    return acc / den


def _ctx_attn_kernel(qa_ref, ka_ref, va_ref, qc_ref, kc_ref, vc_ref, oa_ref, oc_ref):
    for h in range(NA_HEADS):
        k = ka_ref[h].astype(BF16)
        v = va_ref[h].astype(BF16)
        oa_ref[h] = _softmax_pv([(_dot_nt(qa_ref[h], k), v)], exp=jnp.exp2).astype(BF16)
    for h in range(GQA_HEADS):
        j = h // GQA_GROUP
        k = kc_ref[j].astype(BF16)
        v = vc_ref[j].astype(BF16)
        oc_ref[h] = _softmax_pv([(_dot_nt(qc_ref[h], k), v)], exp=jnp.exp2).astype(BF16)


def _ctx_attention(qa, ka, va, qc, kc, vc):
    n_req, s = ka.shape[0], ka.shape[-2]
    t = qa.shape[1]
    stacked = ka.ndim == 5

    def hm(nh):
        return pl.BlockSpec((nh, s, HEAD_DIM), lambda b: (0, b, 0))

    def pr(nh):
        if stacked:
            last = ka.shape[1] - 1
            return pl.BlockSpec((None, None, nh, s, HEAD_DIM), lambda b: (b, last, 0, 0, 0))
        return pl.BlockSpec((None, nh, s, HEAD_DIM), lambda b: (b, 0, 0, 0))

    return pl.pallas_call(
        _ctx_attn_kernel,
        grid=(n_req,),
        in_specs=[hm(NA_HEADS), pr(NA_HEADS), pr(NA_HEADS), hm(GQA_HEADS), pr(GQA_KV_HEADS), pr(GQA_KV_HEADS)],
        out_specs=[hm(NA_HEADS), hm(GQA_HEADS)],
        out_shape=[jax.ShapeDtypeStruct((NA_HEADS, t, HEAD_DIM), BF16),
                   jax.ShapeDtypeStruct((GQA_HEADS, t, HEAD_DIM), BF16)],
        compiler_params=_cparams("arbitrary"),
        name="ctx_attention",
    )(qa, ka, va, qc, kc, vc)


def _with_ones(v):
    return jnp.concatenate([v, jnp.ones((v.shape[0], DEN_COLS), v.dtype)], axis=-1).astype(BF16)


def _na_lat_kernel(q_ref, k_ref, v_ref, ck_ref, cv_ref, bias_ref, o_ref, s0, s1):
    nq = NA_ROW_BLOCK * GRID_W
    nkeys = NA_KEY_ROWS * GRID_W
    nrb = q_ref.shape[0] // nq
    grid_rows = nrb * NA_ROW_BLOCK
    sub = nq // NA_Q_SPLIT
    hd = q_ref.shape[-1]
    ck = ck_ref[...].astype(BF16)
    cv = _with_ones(cv_ref[...])
    bufs = (s0, s1)
    chains = [(blk, part) for blk in range(nrb) for part in range(NA_Q_SPLIT)]
    tn = (((0,), (0,)), ((), ()))

    def window(blk):
        w0 = min(max(NA_ROW_BLOCK * blk - NA_WIN_H // 2, 0), grid_rows - NA_KEY_ROWS)
        return pl.ds(w0 * GRID_W, nkeys)

    def scores(n):
        blk, part = chains[n]
        q = q_ref[pl.ds(blk * nq + part * sub, sub), :]
        variant = 0 if blk == 0 else (2 if blk == nrb - 1 else 1)
        buf = bufs[n % 2]
        buf[:nkeys, :] = (_dot_nt(k_ref[window(blk), :].astype(BF16), q)
                          + bias_ref[variant, :, part * sub:(part + 1) * sub])
        buf[nkeys:, :] = _dot_nt(ck, q)

    def finish(n):
        blk, part = chains[n]
        s = bufs[n % 2][...]
        p = jnp.exp2(s - jnp.max(s, axis=0, keepdims=True)).astype(BF16)
        acc = (lax.dot_general(_with_ones(v_ref[window(blk), :]), p[:nkeys], tn, preferred_element_type=F32)
               + lax.dot_general(cv, p[nkeys:], tn, preferred_element_type=F32))
        o = (acc[:hd] / acc[hd:hd + 1]).T
        o_ref[pl.ds(blk * nq + part * sub, sub), :] = o.astype(BF16)

    scores(0)
    for n in range(len(chains)):
        if n + 1 < len(chains):
            scores(n + 1)
        finish(n)


def _na_bias_table(rpb, grid_rows):
    nrb = grid_rows // NA_ROW_BLOCK
    nh, n_dr, n_dc = rpb.shape
    kc = jnp.arange(GRID_W)[:, None]
    c = jnp.arange(GRID_W)[None, :]
    cs = jnp.clip(c - NA_WIN_W // 2, 0, GRID_W - NA_WIN_W)
    in_win = (kc >= cs) & (kc < cs + NA_WIN_W)
    pick = (jnp.arange(n_dc)[:, None, None] == (kc - c + NA_WIN_W - 1)[None]).astype(F32)
    toep = jnp.dot(rpb.astype(F32).reshape(nh * n_dr, n_dc), pick.reshape(n_dc, GRID_W * GRID_W),
                   precision=lax.Precision.HIGHEST).reshape(nh, n_dr, GRID_W, GRID_W)
    toep = jnp.where(in_win, toep * LOG2_E, MASK_VALUE)

    plan = []
    for rb in (0, min(1, nrb - 1), nrb - 1):
        r0 = NA_ROW_BLOCK * rb
        w0 = min(max(r0 - NA_WIN_H // 2, 0), grid_rows - NA_KEY_ROWS)
        rows = []
        for r in range(r0, r0 + NA_ROW_BLOCK):
            rs = min(max(r - NA_WIN_H // 2, 0), grid_rows - NA_WIN_H)
            rows.append([kr - r + NA_WIN_H - 1 if rs <= kr < rs + NA_WIN_H else None
                         for kr in range(w0, w0 + NA_KEY_ROWS)])
        plan.append(rows)

    def assemble(toep_ref, o_ref):
        masked = jnp.full((GRID_W, GRID_W), MASK_VALUE, F32)
        for v, rows in enumerate(plan):
            for i, row in enumerate(rows):
                for j, dr in enumerate(row):
                    o_ref[v, j * GRID_W:(j + 1) * GRID_W, i * GRID_W:(i + 1) * GRID_W] = (
                        masked if dr is None else toep_ref[dr])

    nq, nk = NA_ROW_BLOCK * GRID_W, NA_KEY_ROWS * GRID_W
    return pl.pallas_call(
        assemble,
        grid=(nh,),
        in_specs=[pl.BlockSpec((None, n_dr, GRID_W, GRID_W), lambda h: (h, 0, 0, 0))],
        out_specs=pl.BlockSpec((None, len(plan), nk, nq), lambda h: (h, 0, 0, 0)),
        out_shape=jax.ShapeDtypeStruct((nh, len(plan), nk, nq), F32),
        compiler_params=_cparams("arbitrary"),
        name="na_bias_table",
    )(toep)


def _na_lat_attention(qa, ka, va, cache_k, cache_v, layer, bias, dec_seq):
    t = qa.shape[1]
    dec_batch = t // dec_seq
    grid_rows = dec_seq // GRID_W
    nrb = grid_rows // NA_ROW_BLOCK
    nq = NA_ROW_BLOCK * GRID_W

    assert nrb >= 3
    seqspec = pl.BlockSpec((None, dec_seq, HEAD_DIM), lambda h, b: (h, b, 0))
    cspec = pl.BlockSpec((None, None, None) + cache_k.shape[3:], lambda h, b: (b, layer, h, 0, 0))
    nkeys = NA_KEY_ROWS * GRID_W + cache_k.shape[3]
    return pl.pallas_call(
        _na_lat_kernel,
        grid=(NA_HEADS, dec_batch),
        in_specs=[seqspec, seqspec, seqspec, cspec, cspec,
                  pl.BlockSpec((None,) + bias.shape[1:], lambda h, b: (h, 0, 0, 0))],
        out_specs=seqspec,
        out_shape=jax.ShapeDtypeStruct((NA_HEADS, t, HEAD_DIM), BF16),
        scratch_shapes=[pltpu.VMEM((nkeys, nq // NA_Q_SPLIT), F32)] * 2,
        compiler_params=_cparams("arbitrary", "arbitrary"),
        name="na_lat_attention",
    )(qa, ka, va, cache_k, cache_v, bias)


def _gqa_lat_kernel(q_ref, k_ref, v_ref, ck_ref, cv_ref, o_ref, *scratch):
    _, tq, hd = q_ref.shape
    g = GQA_GROUP
    nh = GQA_KV_HEADS
    ch = GQA_KEY_CHUNK
    nc = k_ref.shape[1] // ch
    s_bufs = [scratch[2 * j:2 * j + 2] for j in range(nh)]
    qs = [q_ref[j * g:(j + 1) * g].reshape(g * tq, hd) for j in range(nh)]

    tn = (((0,), (0,)), ((), ()))

    def scores(j, c):
        return _dot_nt(k_ref[j, pl.ds(c * ch, ch), :].astype(BF16), qs[j])

    def update(j, state, s, v):
        m, acc = state
        m_new = jnp.maximum(m, jnp.max(s, axis=0, keepdims=True))
        p = jnp.exp2(s - m_new).astype(BF16)
        acc = jnp.exp2(m - m_new) * acc + lax.dot_general(_with_ones(v), p, tn, preferred_element_type=F32)
        return m_new, acc

    states = []
    for j in range(nh):
        s = _dot_nt(ck_ref[j].astype(BF16), qs[j])
        m = jnp.max(s, axis=0, keepdims=True)
        p = jnp.exp2(s - m).astype(BF16)
        states.append((m, lax.dot_general(_with_ones(cv_ref[j]), p, tn, preferred_element_type=F32)))
        s_bufs[j][0][...] = scores(j, 0)

    for c in range(nc):
        for j in range(nh):
            if c + 1 < nc:
                s_bufs[j][(c + 1) % 2][...] = scores(j, c + 1)
            v = v_ref[j, pl.ds(c * ch, ch), :]
            states[j] = update(j, states[j], s_bufs[j][c % 2][...], v)
    for j in range(nh):
        m, acc = states[j]
        o = (acc[:hd] / acc[hd:hd + 1]).T
        o_ref[j * g:(j + 1) * g] = o.reshape(g, tq, hd).astype(BF16)


def _gqa_lat_attention(qc, kc, vc, cache_k, cache_v, layer, dec_seq):
    t = qc.shape[1]
    dec_batch = t // dec_seq
    tq = GQA_Q_TILE
    nq = dec_seq // tq
    qspec = pl.BlockSpec((GQA_HEADS, tq, HEAD_DIM), lambda b, i: (0, b * nq + i, 0))
    kvspec = pl.BlockSpec((GQA_KV_HEADS, dec_seq, HEAD_DIM), lambda b, i: (0, b, 0))
    cspec = pl.BlockSpec((None, None) + cache_k.shape[2:], lambda b, i: (b, layer, 0, 0, 0))
    return pl.pallas_call(
        _gqa_lat_kernel,
        grid=(dec_batch, nq),
        in_specs=[qspec, kvspec, kvspec, cspec, cspec],
        out_specs=qspec,
        out_shape=jax.ShapeDtypeStruct((GQA_HEADS, t, HEAD_DIM), BF16),
        scratch_shapes=[pltpu.VMEM((GQA_KEY_CHUNK, GQA_GROUP * tq), F32)] * (2 * GQA_KV_HEADS),
        compiler_params=_cparams("arbitrary", "arbitrary"),
        name="gqa_lat_attention",
    )(qc, kc, vc, cache_k, cache_v)


LRU_CHUNK = 256
LRU_PAD = SUBLANES


def _lru_kernel(xr_ref, gr_ref, h0_ref, cw_ref, cb_ref, w_ref, b_ref, lam_ref,
                y_ref, st_ref, xpad, a_f, u_f, a_b, u_b):
    t, w = xr_ref.shape
    nch = t // LRU_CHUNK
    ngrp = t // SUBLANES

    zeros = jnp.zeros((LRU_PAD, w), F32)
    xpad[pl.ds(0, LRU_PAD), :] = zeros
    xpad[pl.ds(LRU_PAD + t, LRU_PAD), :] = zeros
    xpad[pl.ds(LRU_PAD, t), :] = xr_ref[...]

    lam = lam_ref[...]
    decay = -LRU_C * (jnp.maximum(-lam, 0.0) + jnp.log1p(jnp.exp(-jnp.abs(lam))))
    h0 = h0_ref[...]
    row = lax.broadcasted_iota(I32, (LRU_CHUNK, 1), 0)
    srow = lax.broadcasted_iota(I32, (SUBLANES, 1), 0)
    left = CONV_W // 2

    def gates(c, _):
        t0 = pl.multiple_of(c * LRU_CHUNK, LRU_CHUNK)
        cur = xpad[pl.ds(t0 + LRU_PAD, LRU_CHUNK), :]
        before = xpad[pl.ds(t0, LRU_PAD), :]
        after = xpad[pl.ds(t0 + LRU_PAD + LRU_CHUNK, LRU_PAD), :]
        xc = cb_ref[...]
        for j in range(CONV_W):
            s = left - j
            if s > 0:
                sh = pltpu.roll(cur, s, 0)
                head = jnp.where(srow < s, pltpu.roll(before, s, 0), sh[:SUBLANES])
                tap = jnp.concatenate([head, sh[SUBLANES:]], axis=0)
            elif s < 0:
                sh = pltpu.roll(cur, LRU_CHUNK + s, 0)
                tail = jnp.where(srow >= SUBLANES + s, pltpu.roll(after, SUBLANES + s, 0), sh[-SUBLANES:])
                tap = jnp.concatenate([sh[:-SUBLANES], tail], axis=0)
            else:
                tap = cur
            xc = xc + tap * cw_ref[j:j + 1, :]
        z = _dot(xc.astype(BF16), w_ref[...]) + b_ref[...]
        for d, (a_ref, u_ref, edge_chunk, edge_row) in enumerate(
                ((a_f, u_f, 0, 0), (a_b, u_b, nch - 1, LRU_CHUNK - 1))):
            r = jax.nn.sigmoid(z[:, (2 * d) * w:(2 * d + 1) * w])
            i = jax.nn.sigmoid(z[:, (2 * d + 1) * w:(2 * d + 2) * w])
            log_a = decay[d:d + 1, :] * r
            a = jnp.exp(log_a)
            u = jnp.sqrt(-jnp.tanh(log_a) * (a * a + 1.0)) * (i * xc)
            first = (row == edge_row) & (c == edge_chunk)
            u = u + jnp.where(first, a * h0[d:d + 1, :], 0.0)
            a_ref[pl.ds(t0, LRU_CHUNK), :] = a
            u_ref[pl.ds(t0, LRU_CHUNK), :] = u
        return 0

    lax.fori_loop(0, nch, gates, 0)

    def scan(i, carry):
        hf, hb = carry
        o = pl.multiple_of(i * SUBLANES, SUBLANES)
        a = a_f[pl.ds(o, SUBLANES), :]
        u = u_f[pl.ds(o, SUBLANES), :]
        for s in (1, 2, 4):
            keep = srow >= s
            a_s = jnp.where(keep, pltpu.roll(a, s, 0), 1.0)
            u_s = jnp.where(keep, pltpu.roll(u, s, 0), 0.0)
            u = a * u_s + u
            a = a * a_s
        h = u + a * hf
        u_f[pl.ds(o, SUBLANES), :] = h
        hf = h[SUBLANES - 1:SUBLANES, :]
        o = pl.multiple_of((ngrp - 1 - i) * SUBLANES, SUBLANES)
        a = a_b[pl.ds(o, SUBLANES), :]
        u = u_b[pl.ds(o, SUBLANES), :]
        for s in (1, 2, 4):
            keep = srow < SUBLANES - s
            a_s = jnp.where(keep, pltpu.roll(a, SUBLANES - s, 0), 1.0)
            u_s = jnp.where(keep, pltpu.roll(u, SUBLANES - s, 0), 0.0)
            u = a * u_s + u
            a = a * a_s
        h = u + a * hb
        u_b[pl.ds(o, SUBLANES), :] = h
        hb = h[0:1, :]
        return hf, hb

    zero_row = jnp.zeros((1, w), F32)
    hf, hb = lax.fori_loop(0, ngrp, scan, (zero_row, zero_row))
    st_ref[0:1, :] = hf
    st_ref[1:2, :] = hb

    def emit(c, _):
        t0 = pl.multiple_of(c * LRU_CHUNK, LRU_CHUNK)
        hs = u_f[pl.ds(t0, LRU_CHUNK), :] + u_b[pl.ds(t0, LRU_CHUNK), :]
        y = jax.nn.gelu(gr_ref[pl.ds(t0, LRU_CHUNK), :]) * hs
        y_ref[pl.ds(t0, LRU_CHUNK), :] = y.astype(y_ref.dtype)
        return 0

    lax.fori_loop(0, nch, emit, 0)


def _lru(xr, gr, h0, cw, cb, w_bf, b_cat, lam, t_req):
    t, w = xr.shape
    n_req = t // t_req
    tok = pl.BlockSpec((t_req, w), lambda b: (b, 0))
    st = pl.BlockSpec((None, 2, w), lambda b: (b, 0, 0))
    return pl.pallas_call(
        _lru_kernel,
        grid=(n_req,),
        in_specs=[tok, tok, st, _full(cw), _full(cb), _full(w_bf), _full(b_cat), _full(lam)],
        out_specs=[tok, st],
        out_shape=[jax.ShapeDtypeStruct((t, w), BF16), jax.ShapeDtypeStruct((n_req, 2, w), F32)],
        scratch_shapes=[pltpu.VMEM((t_req + 2 * LRU_PAD, w), F32)] + [pltpu.VMEM((t_req, w), F32)] * 4,
        compiler_params=_cparams("arbitrary"),
        name="rglru",
    )(xr, gr, h0, cw, cb, w_bf, b_cat, lam)


def _outproj_kernel(x_ref, oa_ref, ob_ref, oc_ref, mod_ref, w_ref, gf_ref, rw_ref,
                    xm_ref, u_ref, aff_ref):
    d = x_ref.shape[-1]
    g1 = mod_ref[:, 2 * d:3 * d]
    sh2 = mod_ref[:, 3 * d:4 * d]
    sc2 = mod_ref[:, 4 * d:5 * d]
    rw = rw_ref[...]
    r_hi = rw.astype(BF16)
    r_lo = (rw - r_hi.astype(F32)).astype(BF16)
    for sub in range(x_ref.shape[0] // TOKEN_SUBTILE):
        rows = pl.ds(sub * TOKEN_SUBTILE, TOKEN_SUBTILE)
        pieces = ([oa_ref[h, rows, :].astype(F32) for h in range(NA_HEADS)] + [ob_ref[rows, :].astype(F32)]
                  + [oc_ref[h, rows, :].astype(F32) for h in range(GQA_HEADS)])
        o = jnp.concatenate(pieces, axis=-1).astype(BF16)
        mix = _dot(o, w_ref[...])
        xm = x_ref[rows, :] + g1 * mix
        xm_ref[rows, :] = xm
        u = _rms(xm, gf_ref[...]) * (1.0 + sc2) + sh2
        u_ref[rows, :] = u
        u_hi = u.astype(BF16)
        u_lo = (u - u_hi.astype(F32)).astype(BF16)
        lg = _dot_nt(r_hi, u_hi) + (_dot_nt(r_lo, u_hi) + _dot_nt(r_hi, u_lo))
        m = jnp.max(lg, axis=0, keepdims=True)
        e = jnp.exp(lg - m)
        aff_ref[:, rows] = e / jnp.sum(e, axis=0, keepdims=True)


def _outproj(x, oa, ob, oc, mod, w_bf, gf, rw_t, t_req):
    t, d = x.shape
    tm = TOKEN_TILE
    tpr = max(t_req // tm, 1)
    tok = pl.BlockSpec((tm, d), lambda i: (i, 0))
    hm = pl.BlockSpec((NA_HEADS, tm, HEAD_DIM), lambda i: (0, i, 0))
    return pl.pallas_call(
        _outproj_kernel,
        grid=(t // tm,),
        in_specs=[tok, hm, pl.BlockSpec((tm, LRU_WIDTH), lambda i: (i, 0)), hm,
                  _mod_spec(mod, tpr), _full(w_bf), _full(gf), _full(rw_t)],
        out_specs=[tok, tok, pl.BlockSpec((N_EXPERTS, tm), lambda i: (0, i))],
        out_shape=[jax.ShapeDtypeStruct((t, d), F32), jax.ShapeDtypeStruct((t, d), F32),
                   jax.ShapeDtypeStruct((N_EXPERTS, t), F32)],
        compiler_params=_cparams("arbitrary"),
        name="outproj_router",
    )(x, oa, ob, oc, mod, w_bf, gf, rw_t)


PREFIX_BLOCK = 256
TOPK_REQUESTS_PER_STEP = 8


def _topk_kernel(cap, t_req, compact, aff_ref, *outs):
    ne, tt = aff_ref.shape
    ng = tt // t_req
    slot_ref = outs[0]
    keys = [pltpu.bitcast(aff_ref[:, g * t_req:(g + 1) * t_req], I32) for g in range(ng)]

    def search(it, thrs):
        bit = jnp.left_shift(jnp.int32(1), 30 - it)
        out = []
        for g in range(ng):
            cand = thrs[g] | bit
            cnt = jnp.sum((keys[g] >= cand).astype(F32), axis=-1, keepdims=True)
            out.append(jnp.where(cnt >= cap, cand, thrs[g]))
        return tuple(out)

    thrs = lax.fori_loop(0, 31, search, tuple(jnp.zeros((ne, 1), I32) for _ in range(ng)))

    ri = lax.broadcasted_iota(I32, (PREFIX_BLOCK, PREFIX_BLOCK), 0)
    ci = lax.broadcasted_iota(I32, (PREFIX_BLOCK, PREFIX_BLOCK), 1)
    tri = jnp.where(ri <= ci, 1.0, 0.0).astype(BF16)

    def prefix(mask_f32):
        carry = jnp.zeros((ne, 1), F32)
        blocks = []
        for b in range(t_req // PREFIX_BLOCK):
            blk = mask_f32[:, b * PREFIX_BLOCK:(b + 1) * PREFIX_BLOCK]
            blocks.append(_dot(blk.astype(BF16), tri) + carry)
            carry = carry + jnp.sum(blk, axis=-1, keepdims=True)
        return blocks[0] if len(blocks) == 1 else jnp.concatenate(blocks, axis=-1)

    for g in range(ng):
        gt = keys[g] > thrs[g]
        eq = keys[g] == thrs[g]
        need = cap - jnp.sum(gt.astype(F32), axis=-1, keepdims=True)
        tie_rank = prefix(eq.astype(F32))
        sel = jnp.where(gt, 1.0, jnp.where(eq, (tie_rank <= need).astype(F32), 0.0))
        slot = jnp.where(sel > 0.0, prefix(sel), 0.0)
        slot_ref[:, g * t_req:(g + 1) * t_req] = slot

    if not compact:
        return
    assert ng == 1
    idx_ref, gate_ref = outs[1], outs[2]
    tok = lax.broadcasted_iota(I32, (ne, t_req), 1)
    valid = (slot > 0.0).astype(I32)
    dist = jnp.where(slot > 0.0, tok - (slot.astype(I32) - 1), 0)
    tokv = tok
    gate = aff_ref[...]
    for k in range((t_req - 1).bit_length()):
        sh = 1 << k
        nb = lambda x: pltpu.roll(x, t_req - sh, 1)
        n_valid, n_dist = nb(valid), nb(dist)
        incoming = (n_valid * ((n_dist >> k) & 1)) > 0
        stay = valid * (1 - ((dist >> k) & 1))
        tokv = jnp.where(incoming, nb(tokv), tokv)
        gate = jnp.where(incoming, nb(gate), gate)
        dist = jnp.where(incoming, n_dist, dist)
        valid = jnp.where(incoming, 1, stay)
    idx_ref[...] = tokv[:, :cap]
    gate_ref[...] = gate[:, :cap]


def _topk(aff, t_req, cap, compact):
    ne, t = aff.shape
    n_req = t // t_req
    ng = 1 if compact else min(TOPK_REQUESTS_PER_STEP, n_req)
    assert n_req % ng == 0
    lanes = ng * t_req
    out_specs = [pl.BlockSpec((ne, lanes), lambda b: (0, b))]
    out_shape = [jax.ShapeDtypeStruct((ne, t), F32)]
    if compact:
        out_specs += [pl.BlockSpec((None, ne, cap), lambda b: (b, 0, 0))] * 2
        out_shape += [jax.ShapeDtypeStruct((n_req, ne, cap), I32), jax.ShapeDtypeStruct((n_req, ne, cap), F32)]
    return pl.pallas_call(
        functools.partial(_topk_kernel, cap, t_req, compact),
        grid=(n_req // ng,),
        in_specs=[pl.BlockSpec((ne, lanes), lambda b: (0, b))],
        out_specs=out_specs,
        out_shape=out_shape,
        compiler_params=_cparams("arbitrary"),
        name="expert_topk",
    )(aff)


def _slot_onehot(slot_ref, cap):
    ne = slot_ref.shape[0]
    want = (lax.broadcasted_iota(I32, (cap, 1), 0) + 1).astype(F32)
    return jnp.concatenate([jnp.where(slot_ref[e:e + 1, :] == want, 1.0, 0.0) for e in range(ne)], axis=0)


def _ctx_dispatch_kernel(slot_ref, u_ref, x_ref):
    ne, cap, d = x_ref.shape
    onehot = _slot_onehot(slot_ref, cap).astype(BF16)
    x = _dot(onehot, u_ref[...].astype(BF16))
    x_ref[...] = x.reshape(ne, cap, d).astype(BF16)


def _ctx_dispatch(slot, u, t_req, cap):
    ne, t = slot.shape
    n_req = t // t_req
    d = u.shape[-1]
    return pl.pallas_call(
        _ctx_dispatch_kernel,
        grid=(n_req,),
        in_specs=[pl.BlockSpec((ne, t_req), lambda b: (0, b)),
                  pl.BlockSpec((t_req, d), lambda b: (b, 0))],
        out_specs=pl.BlockSpec((ne, None, cap, d), lambda b: (0, b, 0, 0)),
        out_shape=jax.ShapeDtypeStruct((ne, n_req, cap, d), BF16),
        compiler_params=_cparams("arbitrary"),
        name="ctx_dispatch",
    )(slot, u)


def _ctx_combine_kernel(slot_ref, aff_ref, y_ref, o_ref):
    ne, cap, d = y_ref.shape
    want = (lax.broadcasted_iota(I32, (cap, 1), 0) + 1).astype(F32)
    hots, gates = [], []
    for e in range(ne):
        hot = jnp.where(slot_ref[e:e + 1, :] == want, 1.0, 0.0)
        hots.append(hot)
        gates.append(jnp.sum(hot * aff_ref[e:e + 1, :], axis=-1, keepdims=True))
    onehot = jnp.concatenate(hots, axis=0).astype(BF16)
    y = y_ref[...].reshape(ne * cap, d) * jnp.concatenate(gates, axis=0)
    y_hi = y.astype(BF16)
    y_lo = (y - y_hi.astype(F32)).astype(BF16)
    tn = (((0,), (0,)), ((), ()))
    o_ref[...] = (lax.dot_general(onehot, y_hi, tn, preferred_element_type=F32)
                  + lax.dot_general(onehot, y_lo, tn, preferred_element_type=F32))


def _ctx_combine(slot, aff, ye, t_req):
    ne, n_req, cap, d = ye.shape
    lane = pl.BlockSpec((ne, t_req), lambda b: (0, b))
    return pl.pallas_call(
        _ctx_combine_kernel,
        grid=(n_req,),
        in_specs=[lane, lane, pl.BlockSpec((ne, None, cap, d), lambda b: (0, b, 0, 0))],
        out_specs=pl.BlockSpec((t_req, d), lambda b: (b, 0)),
        out_shape=jax.ShapeDtypeStruct((n_req * t_req, d), F32),
        compiler_params=_cparams("arbitrary"),
        name="ctx_combine",
    )(slot, aff, ye)


SC_GATHER_ROWS = 64


def _lat_dispatch(gidx, u):
    ne, rows = gidx.shape
    d = u.shape[-1]
    total = ne * rows
    mesh = plsc.VectorSubcoreMesh(core_axis_name="c", subcore_axis_name="s")
    n_cores = mesh.num_cores
    n_workers = n_cores * mesh.num_subcores
    chunk = SC_GATHER_ROWS
    per_worker = total // n_workers
    assert total % n_workers == 0 and per_worker % chunk == 0

    @functools.partial(
        pl.kernel, mesh=mesh, out_type=jax.ShapeDtypeStruct((total, d), u.dtype),
        scratch_types=[pltpu.VMEM((chunk,), I32), pltpu.VMEM((chunk, d), u.dtype), pltpu.SemaphoreType.DMA])
    def gather(u_hbm, idx_hbm, out_hbm, idx_v, rows_v, sem):
        worker = lax.axis_index("s") * n_cores + lax.axis_index("c")
        base = worker * per_worker

        @pl.loop(0, per_worker // chunk)
        def _(i):
            off = pl.multiple_of(base + i * chunk, SUBLANES)
            pltpu.sync_copy(idx_hbm.at[pl.ds(off, chunk)], idx_v)
            pltpu.async_copy(u_hbm.at[idx_v], rows_v, sem).wait()
            pltpu.sync_copy(rows_v, out_hbm.at[pl.ds(off, chunk)])

    return gather(u, gidx.reshape(total)).reshape(ne, rows, d)


def _ffn_kernel(nt_ctx, xc_ref, xl_ref, wg_ref, wu_ref, wd_ref, yc_ref, yl_ref, wg_bf, wu_bf, wd_bf):
    j = pl.program_id(1)

    @pl.when(j == 0)
    def _():
        wg_bf[...] = wg_ref[...].astype(BF16)
        wu_bf[...] = wu_ref[...].astype(BF16)
        wd_bf[...] = wd_ref[...].astype(BF16)

    def ffn(x):
        h = jax.nn.silu(_dot(x, wg_bf[...])) * _dot(x, wu_bf[...])
        return _dot(h.astype(BF16), wd_bf[...])

    @pl.when(j < nt_ctx)
    def _():
        yc_ref[...] = ffn(xc_ref[...])

    @pl.when(j >= nt_ctx)
    def _():
        yl_ref[...] = ffn(xl_ref[...].astype(BF16))


def _expert_ffn(x_ctx, x_lat, wg, wu, wd, layer):
    ne, rows_ctx, d = x_ctx.shape
    rows_lat = x_lat.shape[1]
    ff = wg.shape[-1]
    tr = FFN_ROW_TILE
    nt_ctx, nt_lat = rows_ctx // tr, rows_lat // tr
    wspec = lambda a: pl.BlockSpec((None, None) + a.shape[2:], lambda e, j: (layer, e, 0, 0))
    ctx_spec = pl.BlockSpec((None, tr, d), lambda e, j: (e, jnp.minimum(j, nt_ctx - 1), 0))
    lat_spec = pl.BlockSpec((None, tr, d), lambda e, j: (e, jnp.maximum(j - nt_ctx, 0), 0))
    return pl.pallas_call(
        functools.partial(_ffn_kernel, nt_ctx),
        grid=(ne, nt_ctx + nt_lat),
        in_specs=[ctx_spec, lat_spec, wspec(wg), wspec(wu), wspec(wd)],
        out_specs=[ctx_spec, lat_spec],
        out_shape=[jax.ShapeDtypeStruct((ne, rows_ctx, d), F32), jax.ShapeDtypeStruct((ne, rows_lat, d), F32)],
        scratch_shapes=[pltpu.VMEM((d, ff), BF16), pltpu.VMEM((d, ff), BF16), pltpu.VMEM((ff, d), BF16)],
        compiler_params=_cparams("arbitrary", "arbitrary"),
        name="expert_ffn",
    )(x_ctx, x_lat, wg, wu, wd)


COMBINE_GROUP = 4


def _lat_combine_kernel(idx_ref, gate_ref, y_ref, o_ref):
    e = pl.program_id(1)
    cap = y_ref.shape[0]

    @pl.when(e == 0)
    def _():
        o_ref[...] = jnp.zeros(o_ref.shape, o_ref.dtype)

    def body(q, _):
        r0 = q * COMBINE_GROUP
        toks = [idx_ref[0, r0 + k] for k in range(COMBINE_GROUP)]
        acc = [o_ref[pl.ds(i, 1), :] for i in toks]
        for k in range(COMBINE_GROUP):
            o_ref[pl.ds(toks[k], 1), :] = acc[k] + gate_ref[0, r0 + k] * y_ref[pl.ds(r0 + k, 1), :]
        return 0

    lax.fori_loop(0, cap // COMBINE_GROUP, body, 0)


def _lat_combine(idx, gate, ye, t_req):
    n_req, ne, _, cap = idx.shape
    d = ye.shape[-1]
    sspec = pl.BlockSpec((None, None, 1, cap), lambda b, e: (b, e, 0, 0), memory_space=pltpu.SMEM)
    return pl.pallas_call(
        _lat_combine_kernel,
        grid=(n_req, ne),
        in_specs=[sspec, sspec, pl.BlockSpec((None, cap, d), lambda b, e: (e, b, 0))],
        out_specs=pl.BlockSpec((t_req, d), lambda b, e: (b, 0)),
        out_shape=jax.ShapeDtypeStruct((n_req * t_req, d), F32),
        compiler_params=_cparams("arbitrary", "arbitrary"),
        name="lat_combine",
    )(idx, gate, ye)


def _final_kernel(xm_ref, f_ref, mod_ref, o_ref):
    d = xm_ref.shape[-1]
    o_ref[...] = xm_ref[...] + mod_ref[:, 5 * d:6 * d] * f_ref[...]


def _final(xm, ffn, mod, t_req):
    t, d = xm.shape
    tm = TOKEN_TILE
    tpr = max(t_req // tm, 1)
    tok = pl.BlockSpec((tm, d), lambda i: (i, 0))
    return pl.pallas_call(
        _final_kernel,
        grid=(t // tm,),
        in_specs=[tok, tok, _mod_spec(mod, tpr)],
        out_specs=tok,
        out_shape=jax.ShapeDtypeStruct((t, d), F32),
        compiler_params=_cparams("arbitrary"),
        name="final_residual",
    )(xm, ffn, mod)


def _rope_tables(dec_seq):
    pos = jnp.arange(dec_seq)
    n = HEAD_DIM // 4
    inv = ROPE_THETA ** (-jnp.arange(n, dtype=F32) / n)
    ang_r = (pos // GRID_W).astype(F32)[:, None] * inv[None, :]
    ang_c = (pos % GRID_W).astype(F32)[:, None] * inv[None, :]
    cr, sr, cc, sc = jnp.cos(ang_r), jnp.sin(ang_r), jnp.cos(ang_c), jnp.sin(ang_c)
    cos = jnp.concatenate([cr, cr, cc, cc] * 2, axis=-1)
    sin = jnp.concatenate([-sr, sr, -sc, sc] * 2, axis=-1)
    return cos, sin


def _block_diag(wts):
    nb, bw, _ = wts.shape
    eye = jnp.eye(nb, dtype=wts.dtype)
    return (wts[:, :, None, :] * eye[:, None, :, None]).reshape(nb * bw, nb * bw)


def kernel(x_prompt, x_sample, c, cache_na_k, cache_na_v, cache_gqa_k, cache_gqa_v, state_lru, c_ctx, mod_w, mod_b, norm_mix_g, norm_ffn_g, w_in, na_q_norm_g, na_k_norm_g, na_rpb, conv_w, conv_b, lru_w_a, lru_b_a, lru_w_i, lru_b_i, lru_lambda, gqa_q_norm_g, gqa_k_norm_g, w_out, router_w, expert_w_gate, expert_w_up, expert_w_down):
    batch, seq, d = x_prompt.shape
    dec_batch, dec_seq, _ = x_sample.shape
    depth = mod_w.shape[0]
    t_ctx = batch * seq
    t_lat = dec_batch * dec_seq
    assert seq == TOKEN_SUBTILE and dec_seq % TOKEN_TILE == 0 and (batch * seq) % TOKEN_TILE == 0
    assert (dec_seq // GRID_W) % NA_ROW_BLOCK == 0 and dec_seq // GRID_W >= NA_KEY_ROWS
    cap_ctx = EC_FACTOR * seq // N_EXPERTS
    cap_lat = EC_FACTOR * dec_seq // N_EXPERTS
    rows_ctx = batch * cap_ctx
    rows_lat = dec_batch * cap_lat
    assert rows_ctx % FFN_ROW_TILE == 0 and rows_lat % FFN_ROW_TILE == 0 and cap_lat % COMBINE_GROUP == 0

    n_mod_rows = SUBLANES
    assert 1 + dec_batch <= n_mod_rows
    cvec = jnp.zeros((n_mod_rows, d), F32).at[0].set(c_ctx).at[1:1 + dec_batch].set(c)
    mods = _modulation(cvec, mod_w, mod_b)

    rope_tabs = _rope_tables(dec_seq)
    x_c = x_prompt.reshape(t_ctx, d)
    x_l = x_sample.reshape(t_lat, d)
    zeros_h0 = jnp.zeros((batch, 2, LRU_WIDTH), F32)

    new_na_k, new_na_v, new_gqa_k, new_gqa_v, new_lru = [], [], [], [], []
    prev_c = prev_l = None
    for l in range(depth):
        mod_c = mods[l, 0:1].reshape(1, 1, 6 * d)
        mod_l = mods[l, 1:1 + dec_batch].reshape(dec_batch, 1, 6 * d)
        tile2 = lambda g: jnp.concatenate([g, g])
        gains = jnp.stack([tile2(na_q_norm_g[l]), tile2(na_k_norm_g[l]),
                           tile2(gqa_q_norm_g[l]), tile2(gqa_k_norm_g[l])])
        w_in_bf = w_in[l].astype(BF16)
        g_mix = norm_mix_g[l][None, :]
        kv_old = [new_na_k, new_na_v, new_gqa_k, new_gqa_v] if (l == depth - 1 and depth > 1) else None
        x_c, (qa_c, ka_c, va_c, xr_c, gr_c, qc_c, kc_c, vc_c) = _inproj(
            x_c, prev_c, mod_c, g_mix, w_in_bf, gains, None, seq, True, kv_old)
        x_l, (qa_l, ka_l, va_l, xr_l, gr_l, qc_l, kc_l, vc_l) = _inproj(
            x_l, prev_l, mod_l, g_mix, w_in_bf, gains, rope_tabs, dec_seq, False)

        oa_c, oc_c = _ctx_attention(qa_c, ka_c, va_c, qc_c, kc_c, vc_c)
        bias = _na_bias_table(na_rpb[l], dec_seq // GRID_W)
        oa_l = _na_lat_attention(qa_l, ka_l, va_l, cache_na_k, cache_na_v, l, bias, dec_seq)
        oc_l = _gqa_lat_attention(qc_l, kc_l, vc_l, cache_gqa_k, cache_gqa_v, l, dec_seq)

        w_gate = jnp.concatenate([_block_diag(lru_w_a[l, 0]), _block_diag(lru_w_i[l, 0]),
                                  _block_diag(lru_w_a[l, 1]), _block_diag(lru_w_i[l, 1])], axis=1).astype(BF16)
        b_gate = jnp.concatenate([lru_b_a[l, 0], lru_b_i[l, 0], lru_b_a[l, 1], lru_b_i[l, 1]])[None, :]
        lru_args = (conv_w[l], conv_b[l][None, :], w_gate, b_gate, lru_lambda[l])
        ob_c, st = _lru(xr_c, gr_c, zeros_h0, *lru_args, seq)
        ob_l, _ = _lru(xr_l, gr_l, state_lru[:, l], *lru_args, dec_seq)

        w_out_bf = w_out[l].astype(BF16)
        g_ffn = norm_ffn_g[l][None, :]
        rw_t = router_w[l].T
        xm_c, u_c, aff_c = _outproj(x_c, oa_c, ob_c, oc_c, mod_c, w_out_bf, g_ffn, rw_t, seq)
        xm_l, u_l, aff_l = _outproj(x_l, oa_l, ob_l, oc_l, mod_l, w_out_bf, g_ffn, rw_t, dec_seq)

        (slot_c,) = _topk(aff_c, seq, cap_ctx, False)
        _, idx_l, gate_l = _topk(aff_l, dec_seq, cap_lat, True)
        g_l = idx_l + (jnp.arange(dec_batch, dtype=I32) * dec_seq)[:, None, None]
        gidx_l = g_l.transpose(1, 0, 2).reshape(N_EXPERTS, rows_lat)
        idx_l = idx_l.reshape(dec_batch, N_EXPERTS, 1, cap_lat)
        gate_l = gate_l.reshape(dec_batch, N_EXPERTS, 1, cap_lat)

        xe_c = _ctx_dispatch(slot_c, u_c, seq, cap_ctx).reshape(N_EXPERTS, rows_ctx, d)
        xe_l = _lat_dispatch(gidx_l, u_l)
        ye_c, ye_l = _expert_ffn(xe_c, xe_l, expert_w_gate, expert_w_up, expert_w_down, l)
        ffn_c = _ctx_combine(slot_c, aff_c, ye_c.reshape(N_EXPERTS, batch, cap_ctx, d), seq)
        ffn_l = _lat_combine(idx_l, gate_l, ye_l, dec_seq)
        prev_c, prev_l = (ffn_c, mod_c), (ffn_l, mod_l)
        x_c, x_l = xm_c, xm_l

        new_na_k.append(ka_c)
        new_na_v.append(va_c)
        new_gqa_k.append(kc_c)
        new_gqa_v.append(vc_c)
        new_lru.append(st)

    y_p = _final(x_c, prev_c[0], prev_c[1], seq)
    y_s = _final(x_l, prev_l[0], prev_l[1], dec_seq)
    if depth > 1:
        stacked_kv = (new_na_k[-1], new_na_v[-1], new_gqa_k[-1], new_gqa_v[-1])
    else:
        stacked_kv = tuple(a[0][:, None] for a in (new_na_k, new_na_v, new_gqa_k, new_gqa_v))
    return (y_p.reshape(batch, seq, d), y_s.reshape(dec_batch, dec_seq, d), *stacked_kv,
            jnp.stack(new_lru, axis=1))
```

```python
import functools

import jax
import jax.numpy as jnp
from jax import lax
from jax.experimental import pallas as pl
from jax.experimental.pallas import tpu as pltpu
from jax.experimental.pallas import tpu_sc as plsc

F32 = jnp.float32
BF16 = jnp.bfloat16
I32 = jnp.int32

HEAD_DIM = 64
NA_HEADS = 6
GQA_HEADS = 6
GQA_KV_HEADS = 2
GQA_GROUP = GQA_HEADS // GQA_KV_HEADS
LRU_WIDTH = 256
LRU_C = 8.0
CONV_W = 4
GRID_W = 64
NA_WIN_H = 8
NA_WIN_W = 16
N_EXPERTS = 16
EC_FACTOR = 2
ROPE_THETA = 10000.0
EPS = 1e-6
LOG2_E = 1.4426950408889634
NA_DIM = NA_HEADS * HEAD_DIM
GQA_Q_DIM = GQA_HEADS * HEAD_DIM
GQA_KV_DIM = GQA_KV_HEADS * HEAD_DIM
OFF_QA = 0
OFF_KA = OFF_QA + NA_DIM
OFF_VA = OFF_KA + NA_DIM
OFF_XR = OFF_VA + NA_DIM
OFF_GR = OFF_XR + LRU_WIDTH
OFF_QC = OFF_GR + LRU_WIDTH
OFF_KC = OFF_QC + GQA_Q_DIM
OFF_VC = OFF_KC + GQA_KV_DIM

LANES = 128
SUBLANES = 8
VMEM_LIMIT_BYTES = 56 * 1024 * 1024

TOKEN_TILE = 1024
TOKEN_SUBTILE = 256
INPROJ_TILE = 512
NA_ROW_BLOCK = 4
NA_KEY_ROWS = NA_ROW_BLOCK + NA_WIN_H
NA_Q_SPLIT = 1
GQA_Q_TILE = 256
GQA_KEY_CHUNK = 256
DEN_COLS = 16
FFN_ROW_TILE = 512
MASK_VALUE = -1e30


def _cparams(*sem, flags=None):
    return pltpu.CompilerParams(dimension_semantics=sem, vmem_limit_bytes=VMEM_LIMIT_BYTES, flags=flags)


def _dot(a, b):
    return jnp.dot(a, b, preferred_element_type=F32)


def _dot_nt(a, b):
    return lax.dot_general(a, b, (((1,), (1,)), ((), ())), preferred_element_type=F32)


def _rms(x, g):
    ms = jnp.mean(x * x, axis=-1, keepdims=True)
    return x * lax.rsqrt(ms + EPS) * g


def _full(a):
    nd = a.ndim
    return pl.BlockSpec(a.shape, lambda *_: (0,) * nd)


def _mod_spec(mod, tiles_per_request):
    blk = (None, 1, mod.shape[-1])
    if mod.shape[0] == 1:
        return pl.BlockSpec(blk, lambda i: (0, 0, 0))
    return pl.BlockSpec(blk, lambda i: (i // tiles_per_request, 0, 0))


def _mod_kernel(c_ref, w_ref, b_ref, o_ref):
    s = jax.nn.silu(c_ref[...])
    o_ref[...] = jnp.dot(s, w_ref[...], precision=lax.Precision.HIGHEST,
                         preferred_element_type=F32) + b_ref[...]


def _modulation(cvec, mod_w, mod_b):
    depth, d, n = mod_w.shape
    rows = cvec.shape[0]
    tn = 1536
    return pl.pallas_call(
        _mod_kernel,
        grid=(depth, n // tn),
        in_specs=[
            pl.BlockSpec((rows, d), lambda l, j: (0, 0)),
            pl.BlockSpec((None, d, tn), lambda l, j: (l, 0, j)),
            pl.BlockSpec((None, 1, tn), lambda l, j: (l, 0, j)),
        ],
        out_specs=pl.BlockSpec((None, rows, tn), lambda l, j: (l, 0, j)),
        out_shape=jax.ShapeDtypeStruct((depth, rows, n), F32),
        compiler_params=_cparams("arbitrary", "arbitrary"),
        name="modulation",
    )(cvec, mod_w, mod_b.reshape(depth, 1, n))


def _pair_rms(z, gain, head_ones):
    z2 = z * z
    hi = z2.astype(BF16)
    lo = (z2 - hi.astype(F32)).astype(BF16)
    ss = _dot(hi, head_ones) + _dot(lo, head_ones)
    return z * lax.rsqrt(ss * (1.0 / HEAD_DIM) + EPS) * gain


def _pair_rope(z, cos, sin_signed, even_blk):
    nxt = pltpu.roll(z, LANES - 16, 1)
    prv = pltpu.roll(z, 16, 1)
    return z * cos + jnp.where(even_blk, nxt, prv) * sin_signed


def _inproj_kernel(has_prev, rope, kv_per_request, n_kv_layers, *refs):
    if has_prev:
        f_ref, mprev_ref, refs = refs[0], refs[1], refs[2:]
    x_ref, mod_ref, g_ref, w_ref, gains_ref = refs[:5]
    refs = refs[5:]
    if rope:
        cos_ref, sin_ref, refs = refs[0], refs[1], refs[2:]
    n_old = 4 * max(n_kv_layers - 1, 0)
    kv_old, refs = refs[:n_old], refs[n_old:]
    if has_prev:
        x_out, refs = refs[0], refs[1:]
    qa_o, ka_o, va_o, xr_o, gr_o, qc_o, kc_o, vc_o = refs
    for kind, o_ref in enumerate((ka_o, va_o, kc_o, vc_o)):
        for l in range(n_kv_layers - 1):
            o_ref[:, l] = kv_old[kind * (n_kv_layers - 1) + l][...]
    d = x_ref.shape[-1]
    lane = lax.broadcasted_iota(I32, (1, LANES), 1)
    even_blk = (lane // 16) % 2 == 0
    same_head = (lax.broadcasted_iota(I32, (LANES, LANES), 0) // HEAD_DIM
                 == lax.broadcasted_iota(I32, (LANES, LANES), 1) // HEAD_DIM)
    head_ones = jnp.where(same_head, 1.0, 0.0).astype(BF16)
    scale = HEAD_DIM ** -0.5
    sh1 = mod_ref[:, 0:d]
    sc1 = mod_ref[:, d:2 * d]

    for sub in range(x_ref.shape[0] // TOKEN_SUBTILE):
        rows = pl.ds(sub * TOKEN_SUBTILE, TOKEN_SUBTILE)
        x = x_ref[rows, :]
        if has_prev:
            x = x + mprev_ref[:, 5 * d:6 * d] * f_ref[rows, :]
            x_out[rows, :] = x
        u = _rms(x, g_ref[...]) * (1.0 + sc1) + sh1
        y = _dot(u.astype(BF16), w_ref[...])

        def heads_out(o_ref, off, n_pairs, gain_row, use_rope, mul, dtype, per_request):
            for p in range(n_pairs):
                z = y[:, off + p * LANES: off + (p + 1) * LANES]
                if gain_row is not None:
                    z = _pair_rms(z, gains_ref[gain_row:gain_row + 1, :], head_ones)
                if use_rope:
                    z = _pair_rope(z, cos_ref[rows, :], sin_ref[rows, :], even_blk)
                if mul != 1.0:
                    z = z * mul
                for half, zh in enumerate((z[:, :HEAD_DIM], z[:, HEAD_DIM:])):
                    if per_request and n_kv_layers:
                        o_ref[sub, n_kv_layers - 1, 2 * p + half] = zh.astype(dtype)
                    elif per_request:
                        o_ref[sub, 2 * p + half] = zh.astype(dtype)
                    else:
                        o_ref[2 * p + half, rows, :] = zh.astype(dtype)

        heads_out(qa_o, OFF_QA, NA_HEADS // 2, 0, False, scale * LOG2_E, BF16, False)
        heads_out(ka_o, OFF_KA, NA_HEADS // 2, 1, False, 1.0, F32, kv_per_request)
        heads_out(va_o, OFF_VA, NA_HEADS // 2, None, False, 1.0, F32, kv_per_request)
        xr_o[rows, :] = y[:, OFF_XR:OFF_XR + LRU_WIDTH]
        gr_o[rows, :] = y[:, OFF_GR:OFF_GR + LRU_WIDTH]
        heads_out(qc_o, OFF_QC, GQA_HEADS // 2, 2, rope, scale * LOG2_E, BF16, False)
        heads_out(kc_o, OFF_KC, GQA_KV_HEADS // 2, 3, rope, 1.0, F32, kv_per_request)
        heads_out(vc_o, OFF_VC, GQA_KV_HEADS // 2, None, False, 1.0, F32, kv_per_request)


def _inproj(x, prev, mod, g, w_bf, gains, rope_tabs, t_req, kv_per_request, kv_old=None):
    t, d = x.shape
    tm = INPROJ_TILE
    nsub = tm // TOKEN_SUBTILE
    n_req = t // t_req
    tok = pl.BlockSpec((tm, d), lambda i: (i, 0))
    modspec = _mod_spec(mod, max(t_req // tm, 1))
    in_specs = [tok, modspec, _full(g), _full(w_bf), _full(gains)]
    args = [x, mod, g, w_bf, gains]
    rope = rope_tabs is not None
    if rope:
        tpr = t_req // tm
        in_specs += [pl.BlockSpec((tm, LANES), lambda i: (i % tpr, 0))] * 2
        args += list(rope_tabs)
    n_kv_layers = 0
    if kv_old is not None:
        assert kv_per_request
        n_kv_layers = len(kv_old[0]) + 1
        for olds in kv_old:
            for a in olds:
                in_specs.append(pl.BlockSpec((nsub,) + a.shape[1:], lambda i: (i, 0, 0, 0)))
                args.append(a)
    has_prev = prev is not None
    if has_prev:
        ffn, mod_prev = prev
        in_specs = [tok, modspec] + in_specs
        args = [ffn, mod_prev] + args

    def hm(nh, dtype):
        return (pl.BlockSpec((nh, tm, HEAD_DIM), lambda i: (0, i, 0)),
                jax.ShapeDtypeStruct((nh, t, HEAD_DIM), dtype))

    def kv(nh):
        if not kv_per_request:
            return hm(nh, F32)
        assert t_req == TOKEN_SUBTILE
        if n_kv_layers:
            return (pl.BlockSpec((nsub, n_kv_layers, nh, t_req, HEAD_DIM), lambda i: (i, 0, 0, 0, 0)),
                    jax.ShapeDtypeStruct((n_req, n_kv_layers, nh, t_req, HEAD_DIM), F32))
        return (pl.BlockSpec((nsub, nh, t_req, HEAD_DIM), lambda i: (i, 0, 0, 0)),
                jax.ShapeDtypeStruct((n_req, nh, t_req, HEAD_DIM), F32))

    def tk(width):
        return (pl.BlockSpec((tm, width), lambda i: (i, 0)), jax.ShapeDtypeStruct((t, width), F32))

    outs = [hm(NA_HEADS, BF16), kv(NA_HEADS), kv(NA_HEADS), tk(LRU_WIDTH), tk(LRU_WIDTH),
            hm(GQA_HEADS, BF16), kv(GQA_KV_HEADS), kv(GQA_KV_HEADS)]
    if has_prev:
        outs = [tk(d)] + outs
    res = pl.pallas_call(
        functools.partial(_inproj_kernel, has_prev, rope, kv_per_request, n_kv_layers),
        grid=(t // tm,),
        in_specs=in_specs,
        out_specs=[o[0] for o in outs],
        out_shape=[o[1] for o in outs],
        compiler_params=_cparams("arbitrary"),
        name="inproj",
    )(*args)
    if has_prev:
        return res[0], res[1:]
    return x, res


def _softmax_pv(parts, exp=jnp.exp):
    m = None
    for s, _ in parts:
        mi = jnp.max(s, axis=-1, keepdims=True)
        m = mi if m is None else jnp.maximum(m, mi)
    den = None
    acc = None
    for s, v in parts:
        p = exp(s - m)
        di = jnp.sum(p, axis=-1, keepdims=True)
        oi = _dot(p.astype(BF16), v)
        den = di if den is None else den + di
        acc = oi if acc is None else acc + oi
    return acc / den


def _ctx_attn_kernel(qa_ref, ka_ref, va_ref, qc_ref, kc_ref, vc_ref, oa_ref, oc_ref):
    for h in range(NA_HEADS):
        k = ka_ref[h].astype(BF16)
        v = va_ref[h].astype(BF16)
        oa_ref[h] = _softmax_pv([(_dot_nt(qa_ref[h], k), v)], exp=jnp.exp2).astype(BF16)
    for h in range(GQA_HEADS):
        j = h // GQA_GROUP
        k = kc_ref[j].astype(BF16)
        v = vc_ref[j].astype(BF16)
        oc_ref[h] = _softmax_pv([(_dot_nt(qc_ref[h], k), v)], exp=jnp.exp2).astype(BF16)


def _ctx_attention(qa, ka, va, qc, kc, vc):
    n_req, s = ka.shape[0], ka.shape[-2]
    t = qa.shape[1]
    stacked = ka.ndim == 5

    def hm(nh):
        return pl.BlockSpec((nh, s, HEAD_DIM), lambda b: (0, b, 0))

    def pr(nh):
        if stacked:
            last = ka.shape[1] - 1
            return pl.BlockSpec((None, None, nh, s, HEAD_DIM), lambda b: (b, last, 0, 0, 0))
        return pl.BlockSpec((None, nh, s, HEAD_DIM), lambda b: (b, 0, 0, 0))

    return pl.pallas_call(
        _ctx_attn_kernel,
        grid=(n_req,),
        in_specs=[hm(NA_HEADS), pr(NA_HEADS), pr(NA_HEADS), hm(GQA_HEADS), pr(GQA_KV_HEADS), pr(GQA_KV_HEADS)],
        out_specs=[hm(NA_HEADS), hm(GQA_HEADS)],
        out_shape=[jax.ShapeDtypeStruct((NA_HEADS, t, HEAD_DIM), BF16),
                   jax.ShapeDtypeStruct((GQA_HEADS, t, HEAD_DIM), BF16)],
        compiler_params=_cparams("arbitrary"),
        name="ctx_attention",
    )(qa, ka, va, qc, kc, vc)


def _with_ones(v):
    return jnp.concatenate([v, jnp.ones((v.shape[0], DEN_COLS), v.dtype)], axis=-1).astype(BF16)


def _na_lat_kernel(q_ref, k_ref, v_ref, ck_ref, cv_ref, bias_ref, o_ref, s0, s1):
    nq = NA_ROW_BLOCK * GRID_W
    nkeys = NA_KEY_ROWS * GRID_W
    nrb = q_ref.shape[0] // nq
    grid_rows = nrb * NA_ROW_BLOCK
    sub = nq // NA_Q_SPLIT
    hd = q_ref.shape[-1]
    ck = ck_ref[...].astype(BF16)
    cv = _with_ones(cv_ref[...])
    bufs = (s0, s1)
    chains = [(blk, part) for blk in range(nrb) for part in range(NA_Q_SPLIT)]
    tn = (((0,), (0,)), ((), ()))

    def window(blk):
        w0 = min(max(NA_ROW_BLOCK * blk - NA_WIN_H // 2, 0), grid_rows - NA_KEY_ROWS)
        return pl.ds(w0 * GRID_W, nkeys)

    def scores(n):
        blk, part = chains[n]
        q = q_ref[pl.ds(blk * nq + part * sub, sub), :]
        variant = 0 if blk == 0 else (2 if blk == nrb - 1 else 1)
        buf = bufs[n % 2]
        buf[:nkeys, :] = (_dot_nt(k_ref[window(blk), :].astype(BF16), q)
                          + bias_ref[variant, :, part * sub:(part + 1) * sub])
        buf[nkeys:, :] = _dot_nt(ck, q)

    def finish(n):
        blk, part = chains[n]
        s = bufs[n % 2][...]
        p = jnp.exp2(s - jnp.max(s, axis=0, keepdims=True)).astype(BF16)
        acc = (lax.dot_general(_with_ones(v_ref[window(blk), :]), p[:nkeys], tn, preferred_element_type=F32)
               + lax.dot_general(cv, p[nkeys:], tn, preferred_element_type=F32))
        o = (acc[:hd] / acc[hd:hd + 1]).T
        o_ref[pl.ds(blk * nq + part * sub, sub), :] = o.astype(BF16)

    scores(0)
    for n in range(len(chains)):
        if n + 1 < len(chains):
            scores(n + 1)
        finish(n)


def _na_bias_table(rpb, grid_rows):
    nrb = grid_rows // NA_ROW_BLOCK
    nh, n_dr, n_dc = rpb.shape
    kc = jnp.arange(GRID_W)[:, None]
    c = jnp.arange(GRID_W)[None, :]
    cs = jnp.clip(c - NA_WIN_W // 2, 0, GRID_W - NA_WIN_W)
    in_win = (kc >= cs) & (kc < cs + NA_WIN_W)
    pick = (jnp.arange(n_dc)[:, None, None] == (kc - c + NA_WIN_W - 1)[None]).astype(F32)
    toep = jnp.dot(rpb.astype(F32).reshape(nh * n_dr, n_dc), pick.reshape(n_dc, GRID_W * GRID_W),
                   precision=lax.Precision.HIGHEST).reshape(nh, n_dr, GRID_W, GRID_W)
    toep = jnp.where(in_win, toep * LOG2_E, MASK_VALUE)

    plan = []
    for rb in (0, min(1, nrb - 1), nrb - 1):
        r0 = NA_ROW_BLOCK * rb
        w0 = min(max(r0 - NA_WIN_H // 2, 0), grid_rows - NA_KEY_ROWS)
        rows = []
        for r in range(r0, r0 + NA_ROW_BLOCK):
            rs = min(max(r - NA_WIN_H // 2, 0), grid_rows - NA_WIN_H)
            rows.append([kr - r + NA_WIN_H - 1 if rs <= kr < rs + NA_WIN_H else None
                         for kr in range(w0, w0 + NA_KEY_ROWS)])
        plan.append(rows)

    def assemble(toep_ref, o_ref):
        masked = jnp.full((GRID_W, GRID_W), MASK_VALUE, F32)
        for v, rows in enumerate(plan):
            for i, row in enumerate(rows):
                for j, dr in enumerate(row):
                    o_ref[v, j * GRID_W:(j + 1) * GRID_W, i * GRID_W:(i + 1) * GRID_W] = (
                        masked if dr is None else toep_ref[dr])

    nq, nk = NA_ROW_BLOCK * GRID_W, NA_KEY_ROWS * GRID_W
    return pl.pallas_call(
        assemble,
        grid=(nh,),
        in_specs=[pl.BlockSpec((None, n_dr, GRID_W, GRID_W), lambda h: (h, 0, 0, 0))],
        out_specs=pl.BlockSpec((None, len(plan), nk, nq), lambda h: (h, 0, 0, 0)),
        out_shape=jax.ShapeDtypeStruct((nh, len(plan), nk, nq), F32),
        compiler_params=_cparams("arbitrary"),
        name="na_bias_table",
    )(toep)


def _na_lat_attention(qa, ka, va, cache_k, cache_v, layer, bias, dec_seq):
    t = qa.shape[1]
    dec_batch = t // dec_seq
    grid_rows = dec_seq // GRID_W
    nrb = grid_rows // NA_ROW_BLOCK
    nq = NA_ROW_BLOCK * GRID_W

    assert nrb >= 3
    seqspec = pl.BlockSpec((None, dec_seq, HEAD_DIM), lambda h, b: (h, b, 0))
    cspec = pl.BlockSpec((None, None, None) + cache_k.shape[3:], lambda h, b: (b, layer, h, 0, 0))
    nkeys = NA_KEY_ROWS * GRID_W + cache_k.shape[3]
    return pl.pallas_call(
        _na_lat_kernel,
        grid=(NA_HEADS, dec_batch),
        in_specs=[seqspec, seqspec, seqspec, cspec, cspec,
                  pl.BlockSpec((None,) + bias.shape[1:], lambda h, b: (h, 0, 0, 0))],
        out_specs=seqspec,
        out_shape=jax.ShapeDtypeStruct((NA_HEADS, t, HEAD_DIM), BF16),
        scratch_shapes=[pltpu.VMEM((nkeys, nq // NA_Q_SPLIT), F32)] * 2,
        compiler_params=_cparams("arbitrary", "arbitrary"),
        name="na_lat_attention",
    )(qa, ka, va, cache_k, cache_v, bias)


def _gqa_lat_kernel(q_ref, k_ref, v_ref, ck_ref, cv_ref, o_ref, *scratch):
    _, tq, hd = q_ref.shape
    g = GQA_GROUP
    nh = GQA_KV_HEADS
    ch = GQA_KEY_CHUNK
    nc = k_ref.shape[1] // ch
    s_bufs = [scratch[2 * j:2 * j + 2] for j in range(nh)]
    qs = [q_ref[j * g:(j + 1) * g].reshape(g * tq, hd) for j in range(nh)]

    tn = (((0,), (0,)), ((), ()))

    def scores(j, c):
        return _dot_nt(k_ref[j, pl.ds(c * ch, ch), :].astype(BF16), qs[j])

    def update(j, state, s, v):
        m, acc = state
        m_new = jnp.maximum(m, jnp.max(s, axis=0, keepdims=True))
        p = jnp.exp2(s - m_new).astype(BF16)
        acc = jnp.exp2(m - m_new) * acc + lax.dot_general(_with_ones(v), p, tn, preferred_element_type=F32)
        return m_new, acc

    states = []
    for j in range(nh):
        s = _dot_nt(ck_ref[j].astype(BF16), qs[j])
        m = jnp.max(s, axis=0, keepdims=True)
        p = jnp.exp2(s - m).astype(BF16)
        states.append((m, lax.dot_general(_with_ones(cv_ref[j]), p, tn, preferred_element_type=F32)))
        s_bufs[j][0][...] = scores(j, 0)

    for c in range(nc):
        for j in range(nh):
            if c + 1 < nc:
                s_bufs[j][(c + 1) % 2][...] = scores(j, c + 1)
            v = v_ref[j, pl.ds(c * ch, ch), :]
            states[j] = update(j, states[j], s_bufs[j][c % 2][...], v)
    for j in range(nh):
        m, acc = states[j]
        o = (acc[:hd] / acc[hd:hd + 1]).T
        o_ref[j * g:(j + 1) * g] = o.reshape(g, tq, hd).astype(BF16)


def _gqa_lat_attention(qc, kc, vc, cache_k, cache_v, layer, dec_seq):
    t = qc.shape[1]
    dec_batch = t // dec_seq
    tq = GQA_Q_TILE
    nq = dec_seq // tq
    qspec = pl.BlockSpec((GQA_HEADS, tq, HEAD_DIM), lambda b, i: (0, b * nq + i, 0))
    kvspec = pl.BlockSpec((GQA_KV_HEADS, dec_seq, HEAD_DIM), lambda b, i: (0, b, 0))
    cspec = pl.BlockSpec((None, None) + cache_k.shape[2:], lambda b, i: (b, layer, 0, 0, 0))
    return pl.pallas_call(
        _gqa_lat_kernel,
        grid=(dec_batch, nq),
        in_specs=[qspec, kvspec, kvspec, cspec, cspec],
        out_specs=qspec,
        out_shape=jax.ShapeDtypeStruct((GQA_HEADS, t, HEAD_DIM), BF16),
        scratch_shapes=[pltpu.VMEM((GQA_KEY_CHUNK, GQA_GROUP * tq), F32)] * (2 * GQA_KV_HEADS),
        compiler_params=_cparams("arbitrary", "arbitrary"),
        name="gqa_lat_attention",
    )(qc, kc, vc, cache_k, cache_v)


LRU_CHUNK = 256
LRU_PAD = SUBLANES


def _lru_kernel(xr_ref, gr_ref, h0_ref, cw_ref, cb_ref, w_ref, b_ref, lam_ref,
                y_ref, st_ref, xpad, a_f, u_f, a_b, u_b):
    t, w = xr_ref.shape
    nch = t // LRU_CHUNK
    ngrp = t // SUBLANES

    zeros = jnp.zeros((LRU_PAD, w), F32)
    xpad[pl.ds(0, LRU_PAD), :] = zeros
    xpad[pl.ds(LRU_PAD + t, LRU_PAD), :] = zeros
    xpad[pl.ds(LRU_PAD, t), :] = xr_ref[...]

    lam = lam_ref[...]
    decay = -LRU_C * (jnp.maximum(-lam, 0.0) + jnp.log1p(jnp.exp(-jnp.abs(lam))))
    h0 = h0_ref[...]
    row = lax.broadcasted_iota(I32, (LRU_CHUNK, 1), 0)
    srow = lax.broadcasted_iota(I32, (SUBLANES, 1), 0)
    left = CONV_W // 2

    def gates(c, _):
        t0 = pl.multiple_of(c * LRU_CHUNK, LRU_CHUNK)
        cur = xpad[pl.ds(t0 + LRU_PAD, LRU_CHUNK), :]
        before = xpad[pl.ds(t0, LRU_PAD), :]
        after = xpad[pl.ds(t0 + LRU_PAD + LRU_CHUNK, LRU_PAD), :]
        xc = cb_ref[...]
        for j in range(CONV_W):
            s = left - j
            if s > 0:
                sh = pltpu.roll(cur, s, 0)
                head = jnp.where(srow < s, pltpu.roll(before, s, 0), sh[:SUBLANES])
                tap = jnp.concatenate([head, sh[SUBLANES:]], axis=0)
            elif s < 0:
                sh = pltpu.roll(cur, LRU_CHUNK + s, 0)
                tail = jnp.where(srow >= SUBLANES + s, pltpu.roll(after, SUBLANES + s, 0), sh[-SUBLANES:])
                tap = jnp.concatenate([sh[:-SUBLANES], tail], axis=0)
            else:
                tap = cur
            xc = xc + tap * cw_ref[j:j + 1, :]
        z = _dot(xc.astype(BF16), w_ref[...]) + b_ref[...]
        for d, (a_ref, u_ref, edge_chunk, edge_row) in enumerate(
                ((a_f, u_f, 0, 0), (a_b, u_b, nch - 1, LRU_CHUNK - 1))):
            r = 0.5 * (jnp.tanh(0.5 * z[:, (2 * d) * w:(2 * d + 1) * w]) + 1.0)
            i = 0.5 * (jnp.tanh(0.5 * z[:, (2 * d + 1) * w:(2 * d + 2) * w]) + 1.0)
            log_a = decay[d:d + 1, :] * r
            a = jnp.exp(log_a)
            s2 = -jnp.tanh(log_a) * (a * a + 1.0)
            u = jnp.where(s2 > 0.0, s2 * lax.rsqrt(s2), 0.0) * (i * xc)
            first = (row == edge_row) & (c == edge_chunk)
            u = u + jnp.where(first, a * h0[d:d + 1, :], 0.0)
            a_ref[pl.ds(t0, LRU_CHUNK), :] = a
            u_ref[pl.ds(t0, LRU_CHUNK), :] = u
        return 0

    lax.fori_loop(0, nch, gates, 0)

    def scan(i, carry):
        hf, hb = carry
        o = pl.multiple_of(i * SUBLANES, SUBLANES)
        a = a_f[pl.ds(o, SUBLANES), :]
        u = u_f[pl.ds(o, SUBLANES), :]
        for s in (1, 2, 4):
            keep = srow >= s
            a_s = jnp.where(keep, pltpu.roll(a, s, 0), 1.0)
            u_s = jnp.where(keep, pltpu.roll(u, s, 0), 0.0)
            u = a * u_s + u
            a = a * a_s
        h = u + a * hf
        u_f[pl.ds(o, SUBLANES), :] = h
        hf = h[SUBLANES - 1:SUBLANES, :]
        o = pl.multiple_of((ngrp - 1 - i) * SUBLANES, SUBLANES)
        a = a_b[pl.ds(o, SUBLANES), :]
        u = u_b[pl.ds(o, SUBLANES), :]
        for s in (1, 2, 4):
            keep = srow < SUBLANES - s
            a_s = jnp.where(keep, pltpu.roll(a, SUBLANES - s, 0), 1.0)
            u_s = jnp.where(keep, pltpu.roll(u, SUBLANES - s, 0), 0.0)
            u = a * u_s + u
            a = a * a_s
        h = u + a * hb
        u_b[pl.ds(o, SUBLANES), :] = h
        hb = h[0:1, :]
        return hf, hb

    zero_row = jnp.zeros((1, w), F32)
    hf, hb = lax.fori_loop(0, ngrp, scan, (zero_row, zero_row))
    st_ref[0:1, :] = hf
    st_ref[1:2, :] = hb

    def emit(c, _):
        t0 = pl.multiple_of(c * LRU_CHUNK, LRU_CHUNK)
        hs = u_f[pl.ds(t0, LRU_CHUNK), :] + u_b[pl.ds(t0, LRU_CHUNK), :]
        y = jax.nn.gelu(gr_ref[pl.ds(t0, LRU_CHUNK), :]) * hs
        y_ref[pl.ds(t0, LRU_CHUNK), :] = y.astype(y_ref.dtype)
        return 0

    lax.fori_loop(0, nch, emit, 0)


def _lru(xr, gr, h0, cw, cb, w_bf, b_cat, lam, t_req):
    t, w = xr.shape
    n_req = t // t_req
    tok = pl.BlockSpec((t_req, w), lambda b: (b, 0))
    st = pl.BlockSpec((None, 2, w), lambda b: (b, 0, 0))
    return pl.pallas_call(
        _lru_kernel,
        grid=(n_req,),
        in_specs=[tok, tok, st, _full(cw), _full(cb), _full(w_bf), _full(b_cat), _full(lam)],
        out_specs=[tok, st],
        out_shape=[jax.ShapeDtypeStruct((t, w), BF16), jax.ShapeDtypeStruct((n_req, 2, w), F32)],
        scratch_shapes=[pltpu.VMEM((t_req + 2 * LRU_PAD, w), F32)] + [pltpu.VMEM((t_req, w), F32)] * 4,
        compiler_params=_cparams("arbitrary"),
        name="rglru",
    )(xr, gr, h0, cw, cb, w_bf, b_cat, lam)


def _outproj_kernel(x_ref, oa_ref, ob_ref, oc_ref, mod_ref, w_ref, gf_ref, rw_ref,
                    xm_ref, u_ref, aff_ref):
    d = x_ref.shape[-1]
    g1 = mod_ref[:, 2 * d:3 * d]
    sh2 = mod_ref[:, 3 * d:4 * d]
    sc2 = mod_ref[:, 4 * d:5 * d]
    rw = rw_ref[...]
    r_hi = rw.astype(BF16)
    r_lo = (rw - r_hi.astype(F32)).astype(BF16)
    for sub in range(x_ref.shape[0] // TOKEN_SUBTILE):
        rows = pl.ds(sub * TOKEN_SUBTILE, TOKEN_SUBTILE)
        pieces = ([oa_ref[h, rows, :].astype(F32) for h in range(NA_HEADS)] + [ob_ref[rows, :].astype(F32)]
                  + [oc_ref[h, rows, :].astype(F32) for h in range(GQA_HEADS)])
        o = jnp.concatenate(pieces, axis=-1).astype(BF16)
        mix = _dot(o, w_ref[...])
        xm = x_ref[rows, :] + g1 * mix
        xm_ref[rows, :] = xm
        u = _rms(xm, gf_ref[...]) * (1.0 + sc2) + sh2
        u_ref[rows, :] = u
        u_hi = u.astype(BF16)
        u_lo = (u - u_hi.astype(F32)).astype(BF16)
        lg = _dot_nt(r_hi, u_hi) + (_dot_nt(r_lo, u_hi) + _dot_nt(r_hi, u_lo))
        m = jnp.max(lg, axis=0, keepdims=True)
        e = jnp.exp(lg - m)
        aff_ref[:, rows] = e / jnp.sum(e, axis=0, keepdims=True)


def _outproj(x, oa, ob, oc, mod, w_bf, gf, rw_t, t_req):
    t, d = x.shape
    tm = TOKEN_TILE
    tpr = max(t_req // tm, 1)
    tok = pl.BlockSpec((tm, d), lambda i: (i, 0))
    hm = pl.BlockSpec((NA_HEADS, tm, HEAD_DIM), lambda i: (0, i, 0))
    return pl.pallas_call(
        _outproj_kernel,
        grid=(t // tm,),
        in_specs=[tok, hm, pl.BlockSpec((tm, LRU_WIDTH), lambda i: (i, 0)), hm,
                  _mod_spec(mod, tpr), _full(w_bf), _full(gf), _full(rw_t)],
        out_specs=[tok, tok, pl.BlockSpec((N_EXPERTS, tm), lambda i: (0, i))],
        out_shape=[jax.ShapeDtypeStruct((t, d), F32), jax.ShapeDtypeStruct((t, d), F32),
                   jax.ShapeDtypeStruct((N_EXPERTS, t), F32)],
        compiler_params=_cparams("arbitrary"),
        name="outproj_router",
    )(x, oa, ob, oc, mod, w_bf, gf, rw_t)


PREFIX_BLOCK = 256
TOPK_REQUESTS_PER_STEP = 8


def _topk_kernel(cap, t_req, compact, aff_ref, *outs):
    ne, tt = aff_ref.shape
    ng = tt // t_req
    slot_ref = outs[0]
    keys = [pltpu.bitcast(aff_ref[:, g * t_req:(g + 1) * t_req], I32) for g in range(ng)]

    def search(it, thrs):
        bit = jnp.left_shift(jnp.int32(1), 30 - it)
        out = []
        for g in range(ng):
            cand = thrs[g] | bit
            cnt = jnp.sum((keys[g] >= cand).astype(F32), axis=-1, keepdims=True)
            out.append(jnp.where(cnt >= cap, cand, thrs[g]))
        return tuple(out)

    thrs = lax.fori_loop(0, 31, search, tuple(jnp.zeros((ne, 1), I32) for _ in range(ng)))

    ri = lax.broadcasted_iota(I32, (PREFIX_BLOCK, PREFIX_BLOCK), 0)
    ci = lax.broadcasted_iota(I32, (PREFIX_BLOCK, PREFIX_BLOCK), 1)
    tri = jnp.where(ri <= ci, 1.0, 0.0).astype(BF16)

    def prefix(mask_f32):
        carry = jnp.zeros((ne, 1), F32)
        blocks = []
        for b in range(t_req // PREFIX_BLOCK):
            blk = mask_f32[:, b * PREFIX_BLOCK:(b + 1) * PREFIX_BLOCK]
            blocks.append(_dot(blk.astype(BF16), tri) + carry)
            carry = carry + jnp.sum(blk, axis=-1, keepdims=True)
        return blocks[0] if len(blocks) == 1 else jnp.concatenate(blocks, axis=-1)

    for g in range(ng):
        gt = keys[g] > thrs[g]
        eq = keys[g] == thrs[g]
        need = cap - jnp.sum(gt.astype(F32), axis=-1, keepdims=True)
        tie_rank = prefix(eq.astype(F32))
        sel = jnp.where(gt, 1.0, jnp.where(eq, (tie_rank <= need).astype(F32), 0.0))
        slot = jnp.where(sel > 0.0, prefix(sel), 0.0)
        slot_ref[:, g * t_req:(g + 1) * t_req] = slot

    if not compact:
        return
    assert ng == 1
    idx_ref, gate_ref = outs[1], outs[2]
    tok = lax.broadcasted_iota(I32, (ne, t_req), 1)
    valid = (slot > 0.0).astype(I32)
    dist = jnp.where(slot > 0.0, tok - (slot.astype(I32) - 1), 0)
    tokv = tok
    gate = aff_ref[...]
    for k in range((t_req - 1).bit_length()):
        sh = 1 << k
        nb = lambda x: pltpu.roll(x, t_req - sh, 1)
        n_valid, n_dist = nb(valid), nb(dist)
        incoming = (n_valid * ((n_dist >> k) & 1)) > 0
        stay = valid * (1 - ((dist >> k) & 1))
        tokv = jnp.where(incoming, nb(tokv), tokv)
        gate = jnp.where(incoming, nb(gate), gate)
        dist = jnp.where(incoming, n_dist, dist)
        valid = jnp.where(incoming, 1, stay)
    idx_ref[...] = tokv[:, :cap]
    gate_ref[...] = gate[:, :cap]


def _topk(aff, t_req, cap, compact):
    ne, t = aff.shape
    n_req = t // t_req
    ng = 1 if compact else min(TOPK_REQUESTS_PER_STEP, n_req)
    assert n_req % ng == 0
    lanes = ng * t_req
    out_specs = [pl.BlockSpec((ne, lanes), lambda b: (0, b))]
    out_shape = [jax.ShapeDtypeStruct((ne, t), F32)]
    if compact:
        out_specs += [pl.BlockSpec((None, ne, cap), lambda b: (b, 0, 0))] * 2
        out_shape += [jax.ShapeDtypeStruct((n_req, ne, cap), I32), jax.ShapeDtypeStruct((n_req, ne, cap), F32)]
    return pl.pallas_call(
        functools.partial(_topk_kernel, cap, t_req, compact),
        grid=(n_req // ng,),
        in_specs=[pl.BlockSpec((ne, lanes), lambda b: (0, b))],
        out_specs=out_specs,
        out_shape=out_shape,
        compiler_params=_cparams("arbitrary"),
        name="expert_topk",
    )(aff)


def _slot_onehot(slot_ref, cap):
    ne = slot_ref.shape[0]
    want = (lax.broadcasted_iota(I32, (cap, 1), 0) + 1).astype(F32)
    return jnp.concatenate([jnp.where(slot_ref[e:e + 1, :] == want, 1.0, 0.0) for e in range(ne)], axis=0)


def _ctx_dispatch_kernel(slot_ref, u_ref, x_ref):
    ne, cap, d = x_ref.shape
    onehot = _slot_onehot(slot_ref, cap).astype(BF16)
    x = _dot(onehot, u_ref[...].astype(BF16))
    x_ref[...] = x.reshape(ne, cap, d).astype(BF16)


def _ctx_dispatch(slot, u, t_req, cap):
    ne, t = slot.shape
    n_req = t // t_req
    d = u.shape[-1]
    return pl.pallas_call(
        _ctx_dispatch_kernel,
        grid=(n_req,),
        in_specs=[pl.BlockSpec((ne, t_req), lambda b: (0, b)),
                  pl.BlockSpec((t_req, d), lambda b: (b, 0))],
        out_specs=pl.BlockSpec((ne, None, cap, d), lambda b: (0, b, 0, 0)),
        out_shape=jax.ShapeDtypeStruct((ne, n_req, cap, d), BF16),
        compiler_params=_cparams("arbitrary"),
        name="ctx_dispatch",
    )(slot, u)


def _ctx_combine_kernel(slot_ref, aff_ref, y_ref, o_ref):
    ne, cap, d = y_ref.shape
    want = (lax.broadcasted_iota(I32, (cap, 1), 0) + 1).astype(F32)
    hots, gates = [], []
    for e in range(ne):
        hot = jnp.where(slot_ref[e:e + 1, :] == want, 1.0, 0.0)
        hots.append(hot)
        gates.append(jnp.sum(hot * aff_ref[e:e + 1, :], axis=-1, keepdims=True))
    onehot = jnp.concatenate(hots, axis=0).astype(BF16)
    y = y_ref[...].reshape(ne * cap, d) * jnp.concatenate(gates, axis=0)
    y_hi = y.astype(BF16)
    y_lo = (y - y_hi.astype(F32)).astype(BF16)
    tn = (((0,), (0,)), ((), ()))
    o_ref[...] = (lax.dot_general(onehot, y_hi, tn, preferred_element_type=F32)
                  + lax.dot_general(onehot, y_lo, tn, preferred_element_type=F32))


def _ctx_combine(slot, aff, ye, t_req):
    ne, n_req, cap, d = ye.shape
    lane = pl.BlockSpec((ne, t_req), lambda b: (0, b))
    return pl.pallas_call(
        _ctx_combine_kernel,
        grid=(n_req,),
        in_specs=[lane, lane, pl.BlockSpec((ne, None, cap, d), lambda b: (0, b, 0, 0))],
        out_specs=pl.BlockSpec((t_req, d), lambda b: (b, 0)),
        out_shape=jax.ShapeDtypeStruct((n_req * t_req, d), F32),
        compiler_params=_cparams("arbitrary"),
        name="ctx_combine",
    )(slot, aff, ye)


SC_GATHER_ROWS = 64


def _lat_dispatch(gidx, u):
    ne, rows = gidx.shape
    d = u.shape[-1]
    total = ne * rows
    mesh = plsc.VectorSubcoreMesh(core_axis_name="c", subcore_axis_name="s")
    n_cores = mesh.num_cores
    n_workers = n_cores * mesh.num_subcores
    chunk = SC_GATHER_ROWS
    per_worker = total // n_workers
    assert total % n_workers == 0 and per_worker % chunk == 0

    @functools.partial(
        pl.kernel, mesh=mesh, out_type=jax.ShapeDtypeStruct((total, d), u.dtype),
        scratch_types=[pltpu.VMEM((chunk,), I32), pltpu.VMEM((chunk, d), u.dtype), pltpu.SemaphoreType.DMA])
    def gather(u_hbm, idx_hbm, out_hbm, idx_v, rows_v, sem):
        worker = lax.axis_index("s") * n_cores + lax.axis_index("c")
        base = worker * per_worker

        @pl.loop(0, per_worker // chunk)
        def _(i):
            off = pl.multiple_of(base + i * chunk, SUBLANES)
            pltpu.sync_copy(idx_hbm.at[pl.ds(off, chunk)], idx_v)
            pltpu.async_copy(u_hbm.at[idx_v], rows_v, sem).wait()
            pltpu.sync_copy(rows_v, out_hbm.at[pl.ds(off, chunk)])

    return gather(u, gidx.reshape(total)).reshape(ne, rows, d)


def _ffn_kernel(nt_ctx, xc_ref, xl_ref, wg_ref, wu_ref, wd_ref, yc_ref, yl_ref, wg_bf, wu_bf, wd_bf):
    j = pl.program_id(1)

    @pl.when(j == 0)
    def _():
        wg_bf[...] = wg_ref[...].astype(BF16)
        wu_bf[...] = wu_ref[...].astype(BF16)
        wd_bf[...] = wd_ref[...].astype(BF16)

    def ffn(x):
        h = jax.nn.silu(_dot(x, wg_bf[...])) * _dot(x, wu_bf[...])
        return _dot(h.astype(BF16), wd_bf[...])

    @pl.when(j < nt_ctx)
    def _():
        yc_ref[...] = ffn(xc_ref[...])

    @pl.when(j >= nt_ctx)
    def _():
        yl_ref[...] = ffn(xl_ref[...].astype(BF16))


def _expert_ffn(x_ctx, x_lat, wg, wu, wd, layer):
    ne, rows_ctx, d = x_ctx.shape
    rows_lat = x_lat.shape[1]
    ff = wg.shape[-1]
    tr = FFN_ROW_TILE
    nt_ctx, nt_lat = rows_ctx // tr, rows_lat // tr
    wspec = lambda a: pl.BlockSpec((None, None) + a.shape[2:], lambda e, j: (layer, e, 0, 0))
    ctx_spec = pl.BlockSpec((None, tr, d), lambda e, j: (e, jnp.minimum(j, nt_ctx - 1), 0))
    lat_spec = pl.BlockSpec((None, tr, d), lambda e, j: (e, jnp.maximum(j - nt_ctx, 0), 0))
    return pl.pallas_call(
        functools.partial(_ffn_kernel, nt_ctx),
        grid=(ne, nt_ctx + nt_lat),
        in_specs=[ctx_spec, lat_spec, wspec(wg), wspec(wu), wspec(wd)],
        out_specs=[ctx_spec, lat_spec],
        out_shape=[jax.ShapeDtypeStruct((ne, rows_ctx, d), F32), jax.ShapeDtypeStruct((ne, rows_lat, d), F32)],
        scratch_shapes=[pltpu.VMEM((d, ff), BF16), pltpu.VMEM((d, ff), BF16), pltpu.VMEM((ff, d), BF16)],
        compiler_params=_cparams("arbitrary", "arbitrary"),
        name="expert_ffn",
    )(x_ctx, x_lat, wg, wu, wd)


COMBINE_GROUP = 4


def _lat_combine_kernel(idx_ref, gate_ref, y_ref, o_ref):
    e = pl.program_id(1)
    cap = y_ref.shape[0]

    @pl.when(e == 0)
    def _():
        o_ref[...] = jnp.zeros(o_ref.shape, o_ref.dtype)

    def body(q, _):
        r0 = q * COMBINE_GROUP
        toks = [idx_ref[0, r0 + k] for k in range(COMBINE_GROUP)]
        acc = [o_ref[pl.ds(i, 1), :] for i in toks]
        for k in range(COMBINE_GROUP):
            o_ref[pl.ds(toks[k], 1), :] = acc[k] + gate_ref[0, r0 + k] * y_ref[pl.ds(r0 + k, 1), :]
        return 0

    lax.fori_loop(0, cap // COMBINE_GROUP, body, 0)


def _lat_combine(idx, gate, ye, t_req):
    n_req, ne, _, cap = idx.shape
    d = ye.shape[-1]
    sspec = pl.BlockSpec((None, None, 1, cap), lambda b, e: (b, e, 0, 0), memory_space=pltpu.SMEM)
    return pl.pallas_call(
        _lat_combine_kernel,
        grid=(n_req, ne),
        in_specs=[sspec, sspec, pl.BlockSpec((None, cap, d), lambda b, e: (e, b, 0))],
        out_specs=pl.BlockSpec((t_req, d), lambda b, e: (b, 0)),
        out_shape=jax.ShapeDtypeStruct((n_req * t_req, d), F32),
        compiler_params=_cparams("arbitrary", "arbitrary"),
        name="lat_combine",
    )(idx, gate, ye)


def _final_kernel(xm_ref, f_ref, mod_ref, o_ref):
    d = xm_ref.shape[-1]
    o_ref[...] = xm_ref[...] + mod_ref[:, 5 * d:6 * d] * f_ref[...]


def _final(xm, ffn, mod, t_req):
    t, d = xm.shape
    tm = TOKEN_TILE
    tpr = max(t_req // tm, 1)
    tok = pl.BlockSpec((tm, d), lambda i: (i, 0))
    return pl.pallas_call(
        _final_kernel,
        grid=(t // tm,),
        in_specs=[tok, tok, _mod_spec(mod, tpr)],
        out_specs=tok,
        out_shape=jax.ShapeDtypeStruct((t, d), F32),
        compiler_params=_cparams("arbitrary"),
        name="final_residual",
    )(xm, ffn, mod)


def _rope_tables(dec_seq):
    pos = jnp.arange(dec_seq)
    n = HEAD_DIM // 4
    inv = ROPE_THETA ** (-jnp.arange(n, dtype=F32) / n)
    ang_r = (pos // GRID_W).astype(F32)[:, None] * inv[None, :]
    ang_c = (pos % GRID_W).astype(F32)[:, None] * inv[None, :]
    cr, sr, cc, sc = jnp.cos(ang_r), jnp.sin(ang_r), jnp.cos(ang_c), jnp.sin(ang_c)
    cos = jnp.concatenate([cr, cr, cc, cc] * 2, axis=-1)
    sin = jnp.concatenate([-sr, sr, -sc, sc] * 2, axis=-1)
    return cos, sin


def _block_diag(wts):
    nb, bw, _ = wts.shape
    eye = jnp.eye(nb, dtype=wts.dtype)
    return (wts[:, :, None, :] * eye[:, None, :, None]).reshape(nb * bw, nb * bw)


def kernel(x_prompt, x_sample, c, cache_na_k, cache_na_v, cache_gqa_k, cache_gqa_v, state_lru, c_ctx, mod_w, mod_b, norm_mix_g, norm_ffn_g, w_in, na_q_norm_g, na_k_norm_g, na_rpb, conv_w, conv_b, lru_w_a, lru_b_a, lru_w_i, lru_b_i, lru_lambda, gqa_q_norm_g, gqa_k_norm_g, w_out, router_w, expert_w_gate, expert_w_up, expert_w_down):
    batch, seq, d = x_prompt.shape
    dec_batch, dec_seq, _ = x_sample.shape
    depth = mod_w.shape[0]
    t_ctx = batch * seq
    t_lat = dec_batch * dec_seq
    assert seq == TOKEN_SUBTILE and dec_seq % TOKEN_TILE == 0 and (batch * seq) % TOKEN_TILE == 0
    assert (dec_seq // GRID_W) % NA_ROW_BLOCK == 0 and dec_seq // GRID_W >= NA_KEY_ROWS
    cap_ctx = EC_FACTOR * seq // N_EXPERTS
    cap_lat = EC_FACTOR * dec_seq // N_EXPERTS
    rows_ctx = batch * cap_ctx
    rows_lat = dec_batch * cap_lat
    assert rows_ctx % FFN_ROW_TILE == 0 and rows_lat % FFN_ROW_TILE == 0 and cap_lat % COMBINE_GROUP == 0

    n_mod_rows = SUBLANES
    assert 1 + dec_batch <= n_mod_rows
    cvec = jnp.zeros((n_mod_rows, d), F32).at[0].set(c_ctx).at[1:1 + dec_batch].set(c)
    mods = _modulation(cvec, mod_w, mod_b)

    rope_tabs = _rope_tables(dec_seq)
    x_c = x_prompt.reshape(t_ctx, d)
    x_l = x_sample.reshape(t_lat, d)
    zeros_h0 = jnp.zeros((batch, 2, LRU_WIDTH), F32)

    new_na_k, new_na_v, new_gqa_k, new_gqa_v, new_lru = [], [], [], [], []
    prev_c = prev_l = None
    for l in range(depth):
        mod_c = mods[l, 0:1].reshape(1, 1, 6 * d)
        mod_l = mods[l, 1:1 + dec_batch].reshape(dec_batch, 1, 6 * d)
        tile2 = lambda g: jnp.concatenate([g, g])
        gains = jnp.stack([tile2(na_q_norm_g[l]), tile2(na_k_norm_g[l]),
                           tile2(gqa_q_norm_g[l]), tile2(gqa_k_norm_g[l])])
        w_in_bf = w_in[l].astype(BF16)
        g_mix = norm_mix_g[l][None, :]
        kv_old = [new_na_k, new_na_v, new_gqa_k, new_gqa_v] if (l == depth - 1 and depth > 1) else None
        x_c, (qa_c, ka_c, va_c, xr_c, gr_c, qc_c, kc_c, vc_c) = _inproj(
            x_c, prev_c, mod_c, g_mix, w_in_bf, gains, None, seq, True, kv_old)
        x_l, (qa_l, ka_l, va_l, xr_l, gr_l, qc_l, kc_l, vc_l) = _inproj(
            x_l, prev_l, mod_l, g_mix, w_in_bf, gains, rope_tabs, dec_seq, False)

        oa_c, oc_c = _ctx_attention(qa_c, ka_c, va_c, qc_c, kc_c, vc_c)
        bias = _na_bias_table(na_rpb[l], dec_seq // GRID_W)
        oa_l = _na_lat_attention(qa_l, ka_l, va_l, cache_na_k, cache_na_v, l, bias, dec_seq)
        oc_l = _gqa_lat_attention(qc_l, kc_l, vc_l, cache_gqa_k, cache_gqa_v, l, dec_seq)

        w_gate = jnp.concatenate([_block_diag(lru_w_a[l, 0]), _block_diag(lru_w_i[l, 0]),
                                  _block_diag(lru_w_a[l, 1]), _block_diag(lru_w_i[l, 1])], axis=1).astype(BF16)
        b_gate = jnp.concatenate([lru_b_a[l, 0], lru_b_i[l, 0], lru_b_a[l, 1], lru_b_i[l, 1]])[None, :]
        lru_args = (conv_w[l], conv_b[l][None, :], w_gate, b_gate, lru_lambda[l])
        ob_c, st = _lru(xr_c, gr_c, zeros_h0, *lru_args, seq)
        ob_l, _ = _lru(xr_l, gr_l, state_lru[:, l], *lru_args, dec_seq)

        w_out_bf = w_out[l].astype(BF16)
        g_ffn = norm_ffn_g[l][None, :]
        rw_t = router_w[l].T
        xm_c, u_c, aff_c = _outproj(x_c, oa_c, ob_c, oc_c, mod_c, w_out_bf, g_ffn, rw_t, seq)
        xm_l, u_l, aff_l = _outproj(x_l, oa_l, ob_l, oc_l, mod_l, w_out_bf, g_ffn, rw_t, dec_seq)

        (slot_c,) = _topk(aff_c, seq, cap_ctx, False)
        _, idx_l, gate_l = _topk(aff_l, dec_seq, cap_lat, True)
        g_l = idx_l + (jnp.arange(dec_batch, dtype=I32) * dec_seq)[:, None, None]
        gidx_l = g_l.transpose(1, 0, 2).reshape(N_EXPERTS, rows_lat)
        idx_l = idx_l.reshape(dec_batch, N_EXPERTS, 1, cap_lat)
        gate_l = gate_l.reshape(dec_batch, N_EXPERTS, 1, cap_lat)

        xe_c = _ctx_dispatch(slot_c, u_c, seq, cap_ctx).reshape(N_EXPERTS, rows_ctx, d)
        xe_l = _lat_dispatch(gidx_l, u_l)
        ye_c, ye_l = _expert_ffn(xe_c, xe_l, expert_w_gate, expert_w_up, expert_w_down, l)
        ffn_c = _ctx_combine(slot_c, aff_c, ye_c.reshape(N_EXPERTS, batch, cap_ctx, d), seq)
        ffn_l = _lat_combine(idx_l, gate_l, ye_l, dec_seq)
        prev_c, prev_l = (ffn_c, mod_c), (ffn_l, mod_l)
        x_c, x_l = xm_c, xm_l

        new_na_k.append(ka_c)
        new_na_v.append(va_c)
        new_gqa_k.append(kc_c)
        new_gqa_v.append(vc_c)
        new_lru.append(st)

    y_p = _final(x_c, prev_c[0], prev_c[1], seq)
    y_s = _final(x_l, prev_l[0], prev_l[1], dec_seq)
    if depth > 1:
        stacked_kv = (new_na_k[-1], new_na_v[-1], new_gqa_k[-1], new_gqa_v[-1])
    else:
        stacked_kv = tuple(a[0][:, None] for a in (new_na_k, new_na_v, new_gqa_k, new_gqa_v))
    return (y_p.reshape(batch, seq, d), y_s.reshape(dec_batch, dec_seq, d), *stacked_kv,
            jnp.stack(new_lru, axis=1))
```
